```python
import jax, jax.numpy as jnp
from jax import lax
import numpy as np

D_MODEL = 2048
BATCH = 1
SEQ = 8192
DEPTH = 4

GRID_W = 64
CTX_LEN = 256
HEAD_DIM = 128
N_ATTN_HEADS = 8
N_KV_HEADS = 2
N_CONV_GROUPS = 4
N_FOURIER_GROUPS = 4
ATTN_WIDTH = N_ATTN_HEADS * HEAD_DIM
KV_WIDTH = N_KV_HEADS * HEAD_DIM
CONV_WIDTH = N_CONV_GROUPS * HEAD_DIM
FOURIER_WIDTH = N_FOURIER_GROUPS * HEAD_DIM
MIX_WIDTH = ATTN_WIDTH + CONV_WIDTH + FOURIER_WIDTH
Q_END = ATTN_WIDTH
K_END = Q_END + KV_WIDTH
V_END = K_END + KV_WIDTH
CONV_END = V_END + 2 * CONV_WIDTH
IN_WIDTH = CONV_END + FOURIER_WIDTH
CONV_KERNEL = 31
FFN_HIDDEN = 5632
FFN_CONV_KERNEL = 3
Q_BLOCK = 128
ROPE_THETA = 10000.0
NORM_EPS = 1e-6
N_MOD = 6

kernel_name = 'hybrid_dit_attn_conformer_fnet_prefix'


def rmsnorm(x, g):
    xf = x.astype(jnp.float32)
    y = xf * lax.rsqrt(jnp.mean(xf * xf, axis=-1, keepdims=True) + NORM_EPS)
    return (y * g.astype(jnp.float32)).astype(x.dtype)


def layernorm(x, g, b):
    xf = x.astype(jnp.float32)
    mu = jnp.mean(xf, axis=-1, keepdims=True)
    xc = xf - mu
    y = xc * lax.rsqrt(jnp.mean(xc * xc, axis=-1, keepdims=True) + NORM_EPS)
    return (y * g.astype(jnp.float32) + b.astype(jnp.float32)).astype(x.dtype)


def dwconv(x, w):
    k = w.shape[0]
    pad = (k - 1) // 2
    return lax.conv_general_dilated(
        x, w[:, None, :].astype(x.dtype), window_strides=(1,), padding=[(pad, pad)],
        dimension_numbers=('NWC', 'WIO', 'NWC'), feature_group_count=x.shape[-1])


def axial_rope(n_tokens):
    rows = n_tokens // GRID_W
    row = jnp.repeat(jnp.arange(rows), GRID_W).astype(jnp.float32)
    col = jnp.tile(jnp.arange(GRID_W), rows).astype(jnp.float32)
    n_freq = HEAD_DIM // 4
    inv_freq = ROPE_THETA ** (-jnp.arange(n_freq, dtype=jnp.float32) / n_freq)
    ang = jnp.concatenate([row[:, None] * inv_freq, col[:, None] * inv_freq], axis=-1)
    return jnp.cos(ang), jnp.sin(ang)


def apply_rope(x, cos, sin):
    half = HEAD_DIM // 2
    xf = x.astype(jnp.float32)
    x1, x2 = xf[..., :half], xf[..., half:]
    return jnp.concatenate([x1 * cos - x2 * sin, x1 * sin + x2 * cos], axis=-1).astype(x.dtype)


def _heads(t, n):
    b, l, _ = t.shape
    return t.reshape(b, l, n, HEAD_DIM).transpose(0, 2, 1, 3)


def kv_heads(zkv, k_norm, rope):
    k, v = jnp.split(zkv, [KV_WIDTH], axis=-1)
    k = rmsnorm(_heads(k, N_KV_HEADS), k_norm)
    if rope is not None:
        k = apply_rope(k, *rope)
    return k, _heads(v, N_KV_HEADS)


def block_attention(q, keys, vals):
    b, _, l, hd = q.shape
    g = N_ATTN_HEADS // N_KV_HEADS
    nblk = l // Q_BLOCK
    qb = q.reshape(b, N_KV_HEADS, g, nblk, Q_BLOCK, hd).transpose(3, 0, 1, 2, 4, 5)
    scale = HEAD_DIM ** -0.5

    def one_block(qblk):
        s = jnp.einsum('bkgqd,bksd->bkgqs', qblk, keys).astype(jnp.float32) * scale
        p = jax.nn.softmax(s, axis=-1).astype(vals.dtype)
        return jnp.einsum('bkgqs,bksd->bkgqd', p, vals)

    o = lax.map(one_block, qb)
    return o.transpose(1, 2, 3, 0, 4, 5).reshape(b, N_ATTN_HEADS, l, hd)


def conv_module(u, dw, ln_g, ln_b, pw):
    a, gate = jnp.split(u, 2, axis=-1)
    y = a * jax.nn.sigmoid(gate)
    y = dwconv(y, dw)
    y = jax.nn.silu(layernorm(y, ln_g, ln_b))
    return y @ pw


def fourier_mix(f, w):
    b, l, _ = f.shape
    fg = f.astype(jnp.float32).reshape(b, l, N_FOURIER_GROUPS, HEAD_DIM)
    y = jnp.fft.fft2(fg, axes=(1, 3), norm='ortho').real
    return y.astype(f.dtype).reshape(b, l, FOURIER_WIDTH) @ w


def mixer_outputs(z, keys, vals, q_norm, rope, conv_dw, conv_ln_g, conv_ln_b, conv_pw, fourier_w, w_out):
    b, l, _ = z.shape
    q = rmsnorm(_heads(z[..., :Q_END], N_ATTN_HEADS), q_norm)
    if rope is not None:
        q = apply_rope(q, *rope)
    att = block_attention(q, keys, vals).transpose(0, 2, 1, 3).reshape(b, l, ATTN_WIDTH)
    cv = conv_module(z[..., V_END:CONV_END], conv_dw, conv_ln_g, conv_ln_b, conv_pw)
    fo = fourier_mix(z[..., CONV_END:], fourier_w)
    return jnp.concatenate([att, cv, fo], axis=-1) @ w_out


def conv_ffn(h, w_up, dw, w_down):
    u = dwconv(h @ w_up, dw)
    val, gate = jnp.split(u, 2, axis=-1)
    return (jax.nn.silu(gate) * val) @ w_down


def setup_inputs(seed: int = 0) -> dict:
    key = jax.random.key(seed)
    ks = jax.random.split(key, 24)
    D = D_MODEL

    def nrm(k, shape, s):
        return jax.random.normal(k, shape, jnp.float32) * s

    return {
        'x': nrm(ks[0], (BATCH, SEQ, D), 1.0),
        'c': nrm(ks[1], (BATCH, D), 1.0),
        'ctx': nrm(ks[2], (BATCH, CTX_LEN, D), 1.0),
        'c_ctx': nrm(ks[3], (D,), 1.0),
        'w_mod': nrm(ks[4], (DEPTH, D, N_MOD * D), 0.5 * D ** -0.5),
        'b_mod': nrm(ks[5], (DEPTH, N_MOD * D), 0.02),
        'mix_norm': 1.0 + nrm(ks[6], (DEPTH, D), 0.02),
        'ffn_norm': 1.0 + nrm(ks[7], (DEPTH, D), 0.02),
        'w_in': nrm(ks[8], (DEPTH, D, IN_WIDTH), D ** -0.5),
        'q_norm': 1.0 + nrm(ks[9], (DEPTH, HEAD_DIM), 0.02),
        'k_norm': 1.0 + nrm(ks[10], (DEPTH, HEAD_DIM), 0.02),
        'conv_dw': nrm(ks[11], (DEPTH, CONV_KERNEL, CONV_WIDTH), CONV_KERNEL ** -0.5),
        'conv_ln_g': 1.0 + nrm(ks[12], (DEPTH, CONV_WIDTH), 0.02),
        'conv_ln_b': nrm(ks[13], (DEPTH, CONV_WIDTH), 0.02),
        'conv_pw': nrm(ks[14], (DEPTH, CONV_WIDTH, CONV_WIDTH), CONV_WIDTH ** -0.5),
        'fourier_w': nrm(ks[15], (DEPTH, FOURIER_WIDTH, FOURIER_WIDTH), FOURIER_WIDTH ** -0.5),
        'w_out': nrm(ks[16], (DEPTH, MIX_WIDTH, D), MIX_WIDTH ** -0.5),
        'ffn_up': nrm(ks[17], (DEPTH, D, 2 * FFN_HIDDEN), D ** -0.5),
        'ffn_dw': nrm(ks[18], (DEPTH, FFN_CONV_KERNEL, 2 * FFN_HIDDEN), FFN_CONV_KERNEL ** -0.5),
        'ffn_down': nrm(ks[19], (DEPTH, FFN_HIDDEN, D), FFN_HIDDEN ** -0.5),
        'final_norm': 1.0 + nrm(ks[20], (D,), 0.02),
    }


def reference(x, c, ctx, c_ctx, w_mod, b_mod, mix_norm, ffn_norm, w_in, q_norm, k_norm, conv_dw, conv_ln_g, conv_ln_b, conv_pw, fourier_w, w_out, ffn_up, ffn_dw, ffn_down, final_norm):
    rope = axial_rope(x.shape[1])
    xl, xc = x, ctx
    for l in range(DEPTH):
        last = l == DEPTH - 1
        m_l = jnp.split((jax.nn.silu(c) @ w_mod[l] + b_mod[l])[:, None, :], N_MOD, axis=-1)
        m_c = jnp.split(jax.nn.silu(c_ctx) @ w_mod[l] + b_mod[l], N_MOD, axis=-1)
        hl = rmsnorm(xl, mix_norm[l]) * (1 + m_l[1]) + m_l[0]
        hc = rmsnorm(xc, mix_norm[l]) * (1 + m_c[1]) + m_c[0]

        if last:
            k_c, v_c = kv_heads(hc @ w_in[l][:, Q_END:V_END], k_norm[l], None)
        else:
            zc = hc @ w_in[l]
            k_c, v_c = kv_heads(zc[..., Q_END:V_END], k_norm[l], None)

        zl = hl @ w_in[l]
        k_l, v_l = kv_heads(zl[..., Q_END:V_END], k_norm[l], rope)
        keys = jnp.concatenate([k_l, k_c], axis=2)
        vals = jnp.concatenate([v_l, v_c], axis=2)
        xl = xl + m_l[2] * mixer_outputs(zl, keys, vals, q_norm[l], rope, conv_dw[l], conv_ln_g[l],
                                         conv_ln_b[l], conv_pw[l], fourier_w[l], w_out[l])
        hl2 = rmsnorm(xl, ffn_norm[l]) * (1 + m_l[4]) + m_l[3]
        xl = xl + m_l[5] * conv_ffn(hl2, ffn_up[l], ffn_dw[l], ffn_down[l])

        if not last:
            xc = xc + m_c[2] * mixer_outputs(zc, k_c, v_c, q_norm[l], None, conv_dw[l], conv_ln_g[l],
                                             conv_ln_b[l], conv_pw[l], fourier_w[l], w_out[l])
            hc2 = rmsnorm(xc, ffn_norm[l]) * (1 + m_c[4]) + m_c[3]
            xc = xc + m_c[5] * conv_ffn(hc2, ffn_up[l], ffn_dw[l], ffn_down[l])
    return rmsnorm(xl, final_norm)
```

```python
import functools
import math

import numpy as np
import jax
import jax.numpy as jnp
from jax import lax
from jax.experimental import pallas as pl
from jax.experimental.pallas import tpu as pltpu

F32 = jnp.float32
BF16 = jnp.bfloat16

HEAD_DIM = 128
N_ATTN_HEADS = 8
N_KV_HEADS = 2
GROUP = N_ATTN_HEADS // N_KV_HEADS
ATTN_WIDTH = N_ATTN_HEADS * HEAD_DIM
KV_WIDTH = N_KV_HEADS * HEAD_DIM
CONV_WIDTH = 4 * HEAD_DIM
FOURIER_WIDTH = 4 * HEAD_DIM
N_FOURIER_GROUPS = 4
Q_END = ATTN_WIDTH
K_END = Q_END + KV_WIDTH
V_END = K_END + KV_WIDTH
GLU_MID = V_END + CONV_WIDTH
CONV_END = V_END + 2 * CONV_WIDTH
IN_WIDTH = CONV_END + FOURIER_WIDTH
MIX_WIDTH = ATTN_WIDTH + CONV_WIDTH + FOURIER_WIDTH
CONV_KERNEL = 31
CONV_PAD = (CONV_KERNEL - 1) // 2
FFN_CONV_KERNEL = 3
GRID_W = 64
ROPE_THETA = 10000.0
NORM_EPS = 1e-6
N_MOD = 6
LOG2E = 1.4426950408889634

V7X_VMEM_LIMIT_BYTES = 56 * 1024 * 1024
SUBLANES_F32 = 8
HALO = 16
DFT_N1 = 128
DFT_MIN_N2 = 16


def _cparams(sem):
    return pltpu.CompilerParams(dimension_semantics=sem, vmem_limit_bytes=V7X_VMEM_LIMIT_BYTES)


def _resident(shape, index_map):
    return pl.BlockSpec(shape, index_map, pipeline_mode=pl.Buffered(1))


def _sigmoid(x):
    return 1.0 / (1.0 + jnp.exp(-x))


def _pick_tile(n, pref):
    t = min(n, pref)
    while n % t:
        t //= 2
    return t


def _mod_kernel(c_ref, w_ref, b_ref, o_ref):
    cc = c_ref[...]
    s = cc * _sigmoid(cc)
    o_ref[...] = jnp.dot(s, w_ref[...], preferred_element_type=F32,
                         precision=lax.Precision.HIGHEST) + b_ref[...]


def _modulation(cc, w_mod, b_mod):
    depth, d, n = w_mod.shape
    tn = _pick_tile(n, 1024)
    return pl.pallas_call(
        _mod_kernel,
        grid=(depth, n // tn),
        in_specs=[
            pl.BlockSpec((SUBLANES_F32, d), lambda l, j: (0, 0)),
            pl.BlockSpec((None, d, tn), lambda l, j: (l, 0, j)),
            pl.BlockSpec((None, 1, tn), lambda l, j: (l, 0, j)),
        ],
        out_specs=pl.BlockSpec((None, SUBLANES_F32, tn), lambda l, j: (l, 0, j)),
        out_shape=jax.ShapeDtypeStruct((depth, SUBLANES_F32, n), F32),
        compiler_params=_cparams(("arbitrary", "arbitrary")),
        name="adaln_modulation",
    )(cc, w_mod, b_mod.reshape(depth, 1, n))


def _head_norm_rope(z, g, cos, sin):
    y = z * lax.rsqrt(jnp.mean(z * z, axis=-1, keepdims=True) + NORM_EPS) * g
    return y * cos + pltpu.roll(y, HEAD_DIM // 2, axis=1) * sin


def _inproj_kernel(x_ref, g_ref, mod_ref, w_ref, qn_ref, kn_ref, cos_ref, sin_ref, dft_ref,
                   q_ref, kt_ref, v_ref, y_ref, zr_ref, zi_ref, *, row, d, q_scale):
    x = x_ref[...]
    shift = mod_ref[row:row + 1, 0:d]
    scale = mod_ref[row:row + 1, d:2 * d]
    ms = jnp.mean(x * x, axis=-1, keepdims=True)
    h = (x * lax.rsqrt(ms + NORM_EPS) * g_ref[...]) * (1.0 + scale) + shift
    hb = h.astype(BF16)
    cos = cos_ref[...]
    sin = sin_ref[...]

    def proj(lo, hi):
        return jnp.dot(hb, w_ref[:, lo:hi], preferred_element_type=F32)

    for p in range(N_ATTN_HEADS // 2):
        z = proj(2 * p * HEAD_DIM, (2 * p + 2) * HEAD_DIM)
        for j in range(2):
            hd = 2 * p + j
            qh = _head_norm_rope(z[:, j * HEAD_DIM:(j + 1) * HEAD_DIM], qn_ref[...], cos, sin)
            q_ref[:, hd * HEAD_DIM:(hd + 1) * HEAD_DIM] = (qh * q_scale).astype(BF16)
    z = proj(Q_END, K_END)
    for j in range(N_KV_HEADS):
        kh = _head_norm_rope(z[:, j * HEAD_DIM:(j + 1) * HEAD_DIM], kn_ref[...], cos, sin)
        kt_ref[j * HEAD_DIM:(j + 1) * HEAD_DIM, :] = kh.T.astype(BF16)
    v_ref[...] = proj(K_END, V_END).astype(BF16)
    y_ref[...] = proj(V_END, GLU_MID) * _sigmoid(proj(GLU_MID, CONV_END))
    f = proj(CONV_END, IN_WIDTH)
    for g in range(N_FOURIER_GROUPS):
        r = jnp.dot(f[:, g * HEAD_DIM:(g + 1) * HEAD_DIM].astype(BF16), dft_ref[...],
                    preferred_element_type=F32)
        zr_ref[:, g * HEAD_DIM:(g + 1) * HEAD_DIM] = r[:, :HEAD_DIM].astype(BF16)
        zi_ref[:, g * HEAD_DIM:(g + 1) * HEAD_DIM] = r[:, HEAD_DIM:].astype(BF16)


def _in_projection(x, norm_g, mods, w_in, q_norm, k_norm, cos, sin, dft_c, *, layer, row):
    n, d = x.shape
    tm = _pick_tile(n, 512)
    lyr = lambda i: (layer, 0, 0)
    rows = lambda i: (i, 0)
    kern = functools.partial(_inproj_kernel, row=row, d=d, q_scale=HEAD_DIM ** -0.5 * LOG2E)
    return pl.pallas_call(
        kern,
        grid=(n // tm,),
        in_specs=[
            pl.BlockSpec((tm, d), rows),
            pl.BlockSpec((None, 1, d), lyr),
            pl.BlockSpec((None, SUBLANES_F32, N_MOD * d), lyr),
            _resident((None, d, IN_WIDTH), lyr),
            pl.BlockSpec((None, 1, HEAD_DIM), lyr),
            pl.BlockSpec((None, 1, HEAD_DIM), lyr),
            pl.BlockSpec((tm, HEAD_DIM), rows),
            pl.BlockSpec((tm, HEAD_DIM), rows),
            pl.BlockSpec((HEAD_DIM, 2 * HEAD_DIM), lambda i: (0, 0)),
        ],
        out_specs=[
            pl.BlockSpec((tm, ATTN_WIDTH), rows),
            pl.BlockSpec((KV_WIDTH, tm), lambda i: (0, i)),
            pl.BlockSpec((tm, KV_WIDTH), rows),
            pl.BlockSpec((tm, CONV_WIDTH), rows),
            pl.BlockSpec((tm, FOURIER_WIDTH), rows),
            pl.BlockSpec((tm, FOURIER_WIDTH), rows),
        ],
        out_shape=[
            jax.ShapeDtypeStruct((n, ATTN_WIDTH), BF16),
            jax.ShapeDtypeStruct((KV_WIDTH, n), BF16),
            jax.ShapeDtypeStruct((n, KV_WIDTH), BF16),
            jax.ShapeDtypeStruct((n, CONV_WIDTH), F32),
            jax.ShapeDtypeStruct((n, FOURIER_WIDTH), BF16),
            jax.ShapeDtypeStruct((n, FOURIER_WIDTH), BF16),
        ],
        compiler_params=_cparams(("parallel",)),
        name="in_projection",
    )(x, norm_g, mods, w_in, q_norm, k_norm, cos, sin, dft_c)


def _attn_kernel(*refs, tk, n_lat_chunks):
    if n_lat_chunks:
        q_ref, ktc_ref, vc_ref, ktl_ref, vl_ref, o_ref, m_sc, l_sc, acc_sc = refs
    else:
        q_ref, ktc_ref, vc_ref, o_ref = refs
    for h in range(GROUP):
        q = q_ref[:, h * HEAD_DIM:(h + 1) * HEAD_DIM]
        s = jnp.dot(q, ktc_ref[...], preferred_element_type=F32)
        m = jnp.max(s, axis=-1, keepdims=True)
        p = jnp.exp2(s - m)
        l = jnp.sum(p, axis=-1, keepdims=True)
        acc = jnp.dot(p.astype(BF16), vc_ref[...], preferred_element_type=F32)
        if n_lat_chunks:
            m_sc[...] = m
            l_sc[...] = l
            acc_sc[...] = acc

            def body(c, carry):
                off = pl.multiple_of(c * tk, tk)
                s = jnp.dot(q, ktl_ref[:, pl.ds(off, tk)], preferred_element_type=F32)
                m_old = m_sc[...]
                m_new = jnp.maximum(m_old, jnp.max(s, axis=-1, keepdims=True))
                alpha = jnp.exp2(m_old - m_new)
                p = jnp.exp2(s - m_new)
                l_sc[...] = alpha * l_sc[...] + jnp.sum(p, axis=-1, keepdims=True)
                acc_sc[...] = alpha * acc_sc[...] + jnp.dot(
                    p.astype(BF16), vl_ref[pl.ds(off, tk), :], preferred_element_type=F32)
                m_sc[...] = m_new
                return carry

            lax.fori_loop(0, n_lat_chunks, body, 0)
            acc = acc_sc[...]
            l = l_sc[...]
        o_ref[:, h * HEAD_DIM:(h + 1) * HEAD_DIM] = (acc / l).astype(BF16)


def _attention(q, kt_c, v_c, kt_l=None, v_l=None):
    n = q.shape[0]
    nc = v_c.shape[0]
    tq = _pick_tile(n, 512)
    qw = GROUP * HEAD_DIM
    in_specs = [
        pl.BlockSpec((tq, qw), lambda k, i: (i, k)),
        pl.BlockSpec((HEAD_DIM, nc), lambda k, i: (k, 0)),
        pl.BlockSpec((nc, HEAD_DIM), lambda k, i: (0, k)),
    ]
    args = [q, kt_c, v_c]
    scratch = []
    tk, n_chunks = 0, 0
    if kt_l is not None:
        nl = v_l.shape[0]
        tk = _pick_tile(nl, 512)
        n_chunks = nl // tk
        in_specs += [
            pl.BlockSpec((HEAD_DIM, nl), lambda k, i: (k, 0)),
            pl.BlockSpec((nl, HEAD_DIM), lambda k, i: (0, k)),
        ]
        args += [kt_l, v_l]
        scratch = [pltpu.VMEM((tq, 1), F32), pltpu.VMEM((tq, 1), F32), pltpu.VMEM((tq, HEAD_DIM), F32)]
    return pl.pallas_call(
        functools.partial(_attn_kernel, tk=tk, n_lat_chunks=n_chunks),
        grid=(N_KV_HEADS, n // tq),
        in_specs=in_specs,
        out_specs=pl.BlockSpec((tq, qw), lambda k, i: (i, k)),
        out_shape=jax.ShapeDtypeStruct((n, ATTN_WIDTH), BF16),
        scratch_shapes=scratch,
        compiler_params=_cparams(("parallel", "parallel")),
        name="attention",
    )(*args)


def _conv_kernel(yp_ref, yc_ref, yn_ref, dw_ref, g_ref, b_ref, pw_ref, o_ref, ext_sc, act_sc,
                 *, tm, n_tiles, chunk):
    i = pl.program_id(0)
    ext_sc[0, 0:HALO, :] = jnp.where(i > 0, yp_ref[...], 0.0)
    ext_sc[0, HALO:HALO + tm, :] = yc_ref[...]
    ext_sc[0, HALO + tm:2 * HALO + tm, :] = jnp.where(i < n_tiles - 1, yn_ref[...], 0.0)
    n_shift_rows = tm + 2 * HALO - SUBLANES_F32
    for s in range(1, SUBLANES_F32):
        ext_sc[s, 0:n_shift_rows, :] = ext_sc[0, s:s + n_shift_rows, :]

    def body(c, carry):
        r0 = pl.multiple_of(c * chunk, chunk)
        acc = jnp.zeros((chunk, CONV_WIDTH), F32)
        for k in range(CONV_KERNEL):
            off = k + HALO - CONV_PAD
            base = pl.multiple_of(r0 + (off // SUBLANES_F32) * SUBLANES_F32, SUBLANES_F32)
            acc = acc + dw_ref[k:k + 1, :] * ext_sc[off % SUBLANES_F32, pl.ds(base, chunk), :]
        mu = jnp.mean(acc, axis=-1, keepdims=True)
        xc = acc - mu
        var = jnp.mean(xc * xc, axis=-1, keepdims=True)
        yn = xc * lax.rsqrt(var + NORM_EPS) * g_ref[...] + b_ref[...]
        act_sc[pl.ds(r0, chunk), :] = (yn * _sigmoid(yn)).astype(BF16)
        return carry

    lax.fori_loop(0, tm // chunk, body, 0)
    o_ref[...] = jnp.dot(act_sc[...], pw_ref[...], preferred_element_type=F32).astype(BF16)


def _conv_module(y, conv_dw, ln_g, ln_b, conv_pw, *, layer):
    n = y.shape[0]
    tm = _pick_tile(n, 256)
    n_tiles = n // tm
    hb = tm // HALO
    lyr = lambda i: (layer, 0, 0)
    chunk = _pick_tile(tm, 64)
    return pl.pallas_call(
        functools.partial(_conv_kernel, tm=tm, n_tiles=n_tiles, chunk=chunk),
        grid=(n_tiles,),
        in_specs=[
            pl.BlockSpec((HALO, CONV_WIDTH), lambda i: (jnp.maximum(i * hb - 1, 0), 0)),
            pl.BlockSpec((tm, CONV_WIDTH), lambda i: (i, 0)),
            pl.BlockSpec((HALO, CONV_WIDTH), lambda i: (jnp.minimum((i + 1) * hb, n // HALO - 1), 0)),
            pl.BlockSpec((None, CONV_KERNEL, CONV_WIDTH), lyr),
            pl.BlockSpec((None, 1, CONV_WIDTH), lyr),
            pl.BlockSpec((None, 1, CONV_WIDTH), lyr),
            pl.BlockSpec((None, CONV_WIDTH, CONV_WIDTH), lyr),
        ],
        out_specs=pl.BlockSpec((tm, CONV_WIDTH), lambda i: (i, 0)),
        out_shape=jax.ShapeDtypeStruct((n, CONV_WIDTH), BF16),
        scratch_shapes=[
            pltpu.VMEM((SUBLANES_F32, tm + 2 * HALO, CONV_WIDTH), F32),
            pltpu.VMEM((tm, CONV_WIDTH), BF16),
        ],
        compiler_params=_cparams(("parallel",)),
        name="conv_module",
    )(y, y, y, conv_dw, ln_g, ln_b, conv_pw)


def _dft_tables(n_pos):
    f32 = lambda a: jnp.asarray(a.astype(np.float32))
    k = np.arange(HEAD_DIM)
    ang = 2.0 * np.pi * np.outer(k, k) / HEAD_DIM
    chan = np.concatenate([np.cos(ang), -np.sin(ang)], axis=1) / math.sqrt(HEAD_DIM)
    tabs = dict(chan=f32(chan).astype(BF16))
    if n_pos < DFT_N1 * DFT_MIN_N2:
        a = 2.0 * np.pi * np.outer(np.arange(n_pos), np.arange(n_pos)) / n_pos
        dense = np.concatenate([np.cos(a), np.sin(a)], axis=1) / math.sqrt(n_pos)
        tabs.update(dense=f32(dense).astype(BF16))
        return tabs
    n1 = DFT_N1
    n2 = n_pos // n1
    a1 = 2.0 * np.pi * np.outer(np.arange(n1), np.arange(n1)) / n1
    c1, s1 = np.cos(a1) / math.sqrt(n1), np.sin(a1) / math.sqrt(n1)
    m1 = np.block([[c1, s1], [-s1, c1]])
    at = 2.0 * np.pi * np.outer(np.arange(n2), np.arange(n1)) / n_pos
    a2 = 2.0 * np.pi * np.outer(np.arange(n2), np.arange(n2)) / n2
    g2 = np.concatenate([np.cos(a2), np.sin(a2)], axis=1) / math.sqrt(n2)
    twc = jnp.broadcast_to(f32(np.cos(at))[:, :, None], (n2, n1, HEAD_DIM))
    tws = jnp.broadcast_to(f32(np.sin(at))[:, :, None], (n2, n1, HEAD_DIM))
    tabs.update(n1=n1, n2=n2, m1=f32(m1).astype(BF16), twc=twc, tws=tws, g2=f32(g2).astype(BF16))
    return tabs


def _dft_stage1_kernel(zr_ref, zi_ref, m1_ref, twc_ref, tws_ref, br_ref, bi_ref, *, n1, nb):
    for b in range(nb):
        cols = slice(b * FOURIER_WIDTH, (b + 1) * FOURIER_WIDTH)
        z = jnp.concatenate([zr_ref[:, cols], zi_ref[:, cols]], axis=0)
        a = jnp.dot(m1_ref[...], z, preferred_element_type=F32)
        tc = twc_ref[b]
        ts = tws_ref[b]
        for g in range(N_FOURIER_GROUPS):
            lanes = slice(g * HEAD_DIM, (g + 1) * HEAD_DIM)
            ar = a[:n1, lanes]
            ai = a[n1:, lanes]
            br_ref[b, :, lanes] = (ar * tc + ai * ts).astype(BF16)
            bi_ref[b, :, lanes] = (ai * tc - ar * ts).astype(BF16)


def _dft_stage2_kernel(br_ref, bi_ref, g2_ref, y_ref):
    b = jnp.concatenate([br_ref[...], bi_ref[...]], axis=0)
    y_ref[...] = jnp.dot(g2_ref[...], b, preferred_element_type=F32).astype(BF16)


def _dft_dense_kernel(zr_ref, zi_ref, g_ref, y_ref):
    z = jnp.concatenate([zr_ref[...], zi_ref[...]], axis=0)
    y_ref[...] = jnp.dot(g_ref[...], z, preferred_element_type=F32).astype(BF16)


def _position_dft(zr, zi, tabs):
    n, w = zr.shape
    if "dense" in tabs:
        return pl.pallas_call(
            _dft_dense_kernel,
            out_shape=jax.ShapeDtypeStruct((n, w), BF16),
            compiler_params=pltpu.CompilerParams(vmem_limit_bytes=V7X_VMEM_LIMIT_BYTES),
            name="position_dft_dense",
        )(zr, zi, tabs["dense"])
    n1, n2 = tabs["n1"], tabs["n2"]
    nb = _pick_tile(n2, 8)
    zr2 = zr.reshape(n1, n2 * w)
    zi2 = zi.reshape(n1, n2 * w)
    br, bi = pl.pallas_call(
        functools.partial(_dft_stage1_kernel, n1=n1, nb=nb),
        grid=(n2 // nb,),
        in_specs=[
            pl.BlockSpec((n1, nb * w), lambda j: (0, j)),
            pl.BlockSpec((n1, nb * w), lambda j: (0, j)),
            pl.BlockSpec((2 * n1, 2 * n1), lambda j: (0, 0)),
            pl.BlockSpec((nb, n1, HEAD_DIM), lambda j: (j, 0, 0)),
            pl.BlockSpec((nb, n1, HEAD_DIM), lambda j: (j, 0, 0)),
        ],
        out_specs=[pl.BlockSpec((nb, n1, w), lambda j: (j, 0, 0))] * 2,
        out_shape=[jax.ShapeDtypeStruct((n2, n1, w), BF16)] * 2,
        compiler_params=_cparams(("parallel",)),
        name="position_dft_stage1",
    )(zr2, zi2, tabs["m1"], tabs["twc"], tabs["tws"])
    tn = _pick_tile(n1 * w, 8192)
    y = pl.pallas_call(
        _dft_stage2_kernel,
        grid=(n1 * w // tn,),
        in_specs=[
            pl.BlockSpec((n2, tn), lambda j: (0, j)),
            pl.BlockSpec((n2, tn), lambda j: (0, j)),
            pl.BlockSpec((n2, 2 * n2), lambda j: (0, 0)),
        ],
        out_specs=pl.BlockSpec((n2, tn), lambda j: (0, j)),
        out_shape=jax.ShapeDtypeStruct((n2, n1 * w), BF16),
        compiler_params=_cparams(("parallel",)),
        name="position_dft_stage2",
    )(br.reshape(n2, n1 * w), bi.reshape(n2, n1 * w), tabs["g2"])
    return y.reshape(n, w)


def _outproj_kernel(x_ref, att_ref, cv_ref, y_ref, fw_ref, wo_ref, mod_ref, o_ref, *, row, d):
    fo = jnp.dot(y_ref[...], fw_ref[...], preferred_element_type=F32).astype(BF16)
    acc = jnp.dot(att_ref[...], wo_ref[0:ATTN_WIDTH, :], preferred_element_type=F32)
    acc = acc + jnp.dot(cv_ref[...], wo_ref[ATTN_WIDTH:ATTN_WIDTH + CONV_WIDTH, :], preferred_element_type=F32)
    acc = acc + jnp.dot(fo, wo_ref[ATTN_WIDTH + CONV_WIDTH:MIX_WIDTH, :], preferred_element_type=F32)
    gate = mod_ref[row:row + 1, 2 * d:3 * d]
    o_ref[...] = x_ref[...] + gate * acc


def _out_projection(x, att, cv, yf, fourier_w, w_out, mods, *, layer, row):
    n, d = x.shape
    tm = _pick_tile(n, 512)
    lyr = lambda i: (layer, 0, 0)
    rows = lambda i: (i, 0)
    return pl.pallas_call(
        functools.partial(_outproj_kernel, row=row, d=d),
        grid=(n // tm,),
        in_specs=[
            pl.BlockSpec((tm, d), rows),
            pl.BlockSpec((tm, ATTN_WIDTH), rows),
            pl.BlockSpec((tm, CONV_WIDTH), rows),
            pl.BlockSpec((tm, FOURIER_WIDTH), rows),
            _resident((None, FOURIER_WIDTH, FOURIER_WIDTH), lyr),
            _resident((None, MIX_WIDTH, d), lyr),
            pl.BlockSpec((None, SUBLANES_F32, N_MOD * d), lyr),
        ],
        out_specs=pl.BlockSpec((tm, d), rows),
        out_shape=jax.ShapeDtypeStruct((n, d), F32),
        compiler_params=_cparams(("parallel",)),
        name="out_projection",
    )(x, att, cv, yf, fourier_w, w_out, mods)


def _ffn_kernel(xp_ref, x_ref, xn_ref, g_ref, mod_ref, wuv_ref, wug_ref, dwv_ref, dwg_ref, wd_ref,
                o_ref, h_sc, *, row, d, tm, n_tiles, n_hid):
    i = pl.program_id(0)
    j = pl.program_id(1)

    @pl.when(j == 0)
    def _():
        shift = mod_ref[row:row + 1, 3 * d:4 * d]
        scale = mod_ref[row:row + 1, 4 * d:5 * d]

        def norm(x):
            ms = jnp.mean(x * x, axis=-1, keepdims=True)
            return (x * lax.rsqrt(ms + NORM_EPS) * g_ref[...]) * (1.0 + scale) + shift

        h_sc[0:HALO, :] = jnp.where(i > 0, norm(xp_ref[...]), 0.0).astype(BF16)
        h_sc[HALO:HALO + tm, :] = norm(x_ref[...]).astype(BF16)
        h_sc[HALO + tm:2 * HALO + tm, :] = jnp.where(i < n_tiles - 1, norm(xn_ref[...]), 0.0).astype(BF16)

    h = h_sc[...]

    def up_conv(w_ref, dw_ref):
        u = jnp.dot(h, w_ref[...], preferred_element_type=F32)
        return (dw_ref[0:1, :] * u[HALO - 1:HALO - 1 + tm] + dw_ref[1:2, :] * u[HALO:HALO + tm]
                + dw_ref[2:3, :] * u[HALO + 1:HALO + 1 + tm])

    val = up_conv(wuv_ref, dwv_ref)
    gt = up_conv(wug_ref, dwg_ref)
    act = (gt * _sigmoid(gt) * val).astype(BF16)
    contrib = jnp.dot(act, wd_ref[...], preferred_element_type=F32)

    @pl.when(j == 0)
    def _():
        o_ref[...] = contrib

    @pl.when(j > 0)
    def _():
        o_ref[...] += contrib

    @pl.when(j == n_hid - 1)
    def _():
        gate = mod_ref[row:row + 1, 5 * d:6 * d]
        o_ref[...] = x_ref[...] + gate * o_ref[...]


def _conv_ffn(x, norm_g, mods, w_up, ffn_dw, w_down, *, layer, row):
    n, d = x.shape
    hidden = w_down.shape[1]
    tm = _pick_tile(n, 512)
    th = 512 if hidden % 512 == 0 else 256
    n_tiles, n_hid = n // tm, hidden // th
    hb = tm // HALO
    lyr = lambda i, j: (layer, 0, 0)
    return pl.pallas_call(
        functools.partial(_ffn_kernel, row=row, d=d, tm=tm, n_tiles=n_tiles, n_hid=n_hid),
        grid=(n_tiles, n_hid),
        in_specs=[
            pl.BlockSpec((HALO, d), lambda i, j: (jnp.maximum(i * hb - 1, 0), 0)),
            pl.BlockSpec((tm, d), lambda i, j: (i, 0)),
            pl.BlockSpec((HALO, d), lambda i, j: (jnp.minimum((i + 1) * hb, n // HALO - 1), 0)),
            pl.BlockSpec((None, 1, d), lyr),
            pl.BlockSpec((None, SUBLANES_F32, N_MOD * d), lyr),
            pl.BlockSpec((None, d, th), lambda i, j: (layer, 0, j)),
            pl.BlockSpec((None, d, th), lambda i, j: (layer, 0, n_hid + j)),
            pl.BlockSpec((None, FFN_CONV_KERNEL, th), lambda i, j: (layer, 0, j)),
            pl.BlockSpec((None, FFN_CONV_KERNEL, th), lambda i, j: (layer, 0, n_hid + j)),
            pl.BlockSpec((None, th, d), lambda i, j: (layer, j, 0)),
        ],
        out_specs=pl.BlockSpec((tm, d), lambda i, j: (i, 0)),
        out_shape=jax.ShapeDtypeStruct((n, d), F32),
        scratch_shapes=[pltpu.VMEM((tm + 2 * HALO, d), BF16)],
        compiler_params=_cparams(("parallel", "arbitrary")),
        name="conv_ffn",
    )(x, x, x, norm_g, mods, w_up, w_up, ffn_dw, ffn_dw, w_down)


def _final_norm_kernel(x_ref, g_ref, o_ref):
    x = x_ref[...]
    o_ref[...] = x * lax.rsqrt(jnp.mean(x * x, axis=-1, keepdims=True) + NORM_EPS) * g_ref[...]


def _final_norm(x, g):
    n, d = x.shape
    tm = _pick_tile(n, 512)
    return pl.pallas_call(
        _final_norm_kernel,
        grid=(n // tm,),
        in_specs=[pl.BlockSpec((tm, d), lambda i: (i, 0)), pl.BlockSpec((1, d), lambda i: (0, 0))],
        out_specs=pl.BlockSpec((tm, d), lambda i: (i, 0)),
        out_shape=jax.ShapeDtypeStruct((n, d), F32),
        compiler_params=_cparams(("parallel",)),
        name="final_norm",
    )(x, g.reshape(1, d))


def _rope_tables(n_tokens):
    rows = n_tokens // GRID_W
    row = jnp.repeat(jnp.arange(rows), GRID_W).astype(F32)
    col = jnp.tile(jnp.arange(GRID_W), rows).astype(F32)
    n_freq = HEAD_DIM // 4
    inv_freq = ROPE_THETA ** (-jnp.arange(n_freq, dtype=F32) / n_freq)
    ang = jnp.concatenate([row[:, None] * inv_freq, col[:, None] * inv_freq], axis=-1)
    cos, sin = jnp.cos(ang), jnp.sin(ang)
    return jnp.concatenate([cos, cos], axis=-1), jnp.concatenate([-sin, sin], axis=-1)


def kernel(x, c, ctx, c_ctx, w_mod, b_mod, mix_norm, ffn_norm, w_in, q_norm, k_norm, conv_dw, conv_ln_g,
           conv_ln_b, conv_pw, fourier_w, w_out, ffn_up, ffn_dw, ffn_down, final_norm):
    batch, n_lat, d = x.shape
    n_ctx = ctx.shape[1]
    depth = w_mod.shape[0]
    assert batch == 1 and c.shape == (1, d)

    cc = jnp.zeros((SUBLANES_F32, d), F32).at[0].set(c[0]).at[1].set(c_ctx)
    mods = _modulation(cc, w_mod, b_mod)

    w_in_b, w_out_b = w_in.astype(BF16), w_out.astype(BF16)
    ffn_up_b, ffn_down_b = ffn_up.astype(BF16), ffn_down.astype(BF16)
    conv_pw_b, fourier_w_b = conv_pw.astype(BF16), fourier_w.astype(BF16)
    vec = lambda a: a.reshape(depth, 1, a.shape[-1])
    mix_g, ffn_g, qn, kn = vec(mix_norm), vec(ffn_norm), vec(q_norm), vec(k_norm)
    ln_g, ln_b = vec(conv_ln_g), vec(conv_ln_b)

    rope_l = _rope_tables(n_lat)
    rope_c = (jnp.ones((n_ctx, HEAD_DIM), F32), jnp.zeros((n_ctx, HEAD_DIM), F32))
    tabs_l, tabs_c = _dft_tables(n_lat), _dft_tables(n_ctx)

    xl, xc = x[0], ctx[0]
    for l in range(depth):
        last = l == depth - 1
        qc, ktc, vc, yc, zrc, zic = _in_projection(xc, mix_g, mods, w_in_b, qn, kn, *rope_c, tabs_c["chan"],
                                                   layer=l, row=1)
        ql, ktl, vl, yl, zrl, zil = _in_projection(xl, mix_g, mods, w_in_b, qn, kn, *rope_l, tabs_l["chan"],
                                                   layer=l, row=0)
        att = _attention(ql, ktc, vc, ktl, vl)
        cv = _conv_module(yl, conv_dw, ln_g, ln_b, conv_pw_b, layer=l)
        yf = _position_dft(zrl, zil, tabs_l)
        xl = _out_projection(xl, att, cv, yf, fourier_w_b, w_out_b, mods, layer=l, row=0)
        xl = _conv_ffn(xl, ffn_g, mods, ffn_up_b, ffn_dw, ffn_down_b, layer=l, row=0)
        if not last:
            att = _attention(qc, ktc, vc)
            cv = _conv_module(yc, conv_dw, ln_g, ln_b, conv_pw_b, layer=l)
            yf = _position_dft(zrc, zic, tabs_c)
            xc = _out_projection(xc, att, cv, yf, fourier_w_b, w_out_b, mods, layer=l, row=1)
            xc = _conv_ffn(xc, ffn_g, mods, ffn_up_b, ffn_dw, ffn_down_b, layer=l, row=1)
    return _final_norm(xl, final_norm)[None]
```

```python
import functools
import math

import numpy as np
import jax
import jax.numpy as jnp
from jax import lax
from jax.experimental import pallas as pl
from jax.experimental.pallas import tpu as pltpu

F32 = jnp.float32
BF16 = jnp.bfloat16

HEAD_DIM = 128
N_ATTN_HEADS = 8
N_KV_HEADS = 2
GROUP = N_ATTN_HEADS // N_KV_HEADS
ATTN_WIDTH = N_ATTN_HEADS * HEAD_DIM
KV_WIDTH = N_KV_HEADS * HEAD_DIM
CONV_WIDTH = 4 * HEAD_DIM
FOURIER_WIDTH = 4 * HEAD_DIM
N_FOURIER_GROUPS = 4
Q_END = ATTN_WIDTH
K_END = Q_END + KV_WIDTH
V_END = K_END + KV_WIDTH
GLU_MID = V_END + CONV_WIDTH
CONV_END = V_END + 2 * CONV_WIDTH
IN_WIDTH = CONV_END + FOURIER_WIDTH
MIX_WIDTH = ATTN_WIDTH + CONV_WIDTH + FOURIER_WIDTH
CONV_KERNEL = 31
CONV_PAD = (CONV_KERNEL - 1) // 2
FFN_CONV_KERNEL = 3
GRID_W = 64
ROPE_THETA = 10000.0
NORM_EPS = 1e-6
N_MOD = 6
LOG2E = 1.4426950408889634

V7X_VMEM_LIMIT_BYTES = 56 * 1024 * 1024
SUBLANES_F32 = 8
HALO = 16
DFT_N1 = 128
DFT_MIN_N2 = 16
V_ROWS = HEAD_DIM + 16


def _cparams(sem):
    return pltpu.CompilerParams(dimension_semantics=sem, vmem_limit_bytes=V7X_VMEM_LIMIT_BYTES)


def _resident(shape, index_map):
    return pl.BlockSpec(shape, index_map, pipeline_mode=pl.Buffered(1))


def _sigmoid(x):
    return 1.0 / (1.0 + jnp.exp(-x))


def _pick_tile(n, pref):
    t = min(n, pref)
    while n % t:
        t //= 2
    return t


def _mod_kernel(c_ref, w_ref, b_ref, o_ref):
    cc = c_ref[...]
    s = cc * _sigmoid(cc)
    o_ref[...] = jnp.dot(s, w_ref[...], preferred_element_type=F32,
                         precision=lax.Precision.HIGHEST) + b_ref[...]


def _modulation(cc, w_mod, b_mod):
    depth, d, n = w_mod.shape
    tn = _pick_tile(n, 1024)
    return pl.pallas_call(
        _mod_kernel,
        grid=(depth, n // tn),
        in_specs=[
            pl.BlockSpec((SUBLANES_F32, d), lambda l, j: (0, 0)),
            pl.BlockSpec((None, d, tn), lambda l, j: (l, 0, j)),
            pl.BlockSpec((None, 1, tn), lambda l, j: (l, 0, j)),
        ],
        out_specs=pl.BlockSpec((None, SUBLANES_F32, tn), lambda l, j: (l, 0, j)),
        out_shape=jax.ShapeDtypeStruct((depth, SUBLANES_F32, n), F32),
        compiler_params=_cparams(("arbitrary", "arbitrary")),
        name="adaln_modulation",
    )(cc, w_mod, b_mod.reshape(depth, 1, n))


def _head_norm_rope(z, g, cos, sin):
    y = z * lax.rsqrt(jnp.mean(z * z, axis=-1, keepdims=True) + NORM_EPS) * g
    return y * cos + pltpu.roll(y, HEAD_DIM // 2, axis=1) * sin


def _inproj_kernel(x_ref, g_ref, mod_ref, w_ref, qn_ref, kn_ref, cos_ref, sin_ref, dft_ref,
                   qt_ref, k_ref, vt_ref, y_ref, zr_ref, zi_ref, *, row, d, q_scale):
    x = x_ref[...]
    shift = mod_ref[row:row + 1, 0:d]
    scale = mod_ref[row:row + 1, d:2 * d]
    ms = jnp.mean(x * x, axis=-1, keepdims=True)
    h = (x * lax.rsqrt(ms + NORM_EPS) * g_ref[...]) * (1.0 + scale) + shift
    hb = h.astype(BF16)
    cos = cos_ref[...]
    sin = sin_ref[...]

    def proj(lo, hi):
        return jnp.dot(hb, w_ref[:, lo:hi], preferred_element_type=F32)

    for p in range(N_ATTN_HEADS // 2):
        z = proj(2 * p * HEAD_DIM, (2 * p + 2) * HEAD_DIM)
        for j in range(2):
            hd = 2 * p + j
            qh = _head_norm_rope(z[:, j * HEAD_DIM:(j + 1) * HEAD_DIM], qn_ref[...], cos, sin)
            qt_ref[hd * HEAD_DIM:(hd + 1) * HEAD_DIM, :] = (qh * q_scale).T.astype(BF16)
    z = proj(Q_END, K_END)
    for j in range(N_KV_HEADS):
        kh = _head_norm_rope(z[:, j * HEAD_DIM:(j + 1) * HEAD_DIM], kn_ref[...], cos, sin)
        k_ref[:, j * HEAD_DIM:(j + 1) * HEAD_DIM] = kh.astype(BF16)
    vt = proj(K_END, V_END).T
    pad_rows = V_ROWS - HEAD_DIM
    ones_row = (lax.broadcasted_iota(jnp.int32, (pad_rows, vt.shape[1]), 0) == 0).astype(BF16)
    for j in range(N_KV_HEADS):
        vt_ref[j * V_ROWS:j * V_ROWS + HEAD_DIM, :] = vt[j * HEAD_DIM:(j + 1) * HEAD_DIM].astype(BF16)
        vt_ref[j * V_ROWS + HEAD_DIM:(j + 1) * V_ROWS, :] = ones_row
    y_ref[...] = proj(V_END, GLU_MID) * _sigmoid(proj(GLU_MID, CONV_END))
    f = proj(CONV_END, IN_WIDTH)
    for g in range(N_FOURIER_GROUPS):
        r = jnp.dot(f[:, g * HEAD_DIM:(g + 1) * HEAD_DIM].astype(BF16), dft_ref[...],
                    preferred_element_type=F32)
        zr_ref[:, g * HEAD_DIM:(g + 1) * HEAD_DIM] = r[:, :HEAD_DIM].astype(BF16)
        zi_ref[:, g * HEAD_DIM:(g + 1) * HEAD_DIM] = r[:, HEAD_DIM:].astype(BF16)


def _in_projection(x, norm_g, mods, w_in, q_norm, k_norm, cos, sin, dft_c, *, layer, row):
    n, d = x.shape
    tm = _pick_tile(n, 512)
    lyr = lambda i: (layer, 0, 0)
    rows = lambda i: (i, 0)
    kern = functools.partial(_inproj_kernel, row=row, d=d, q_scale=HEAD_DIM ** -0.5 * LOG2E)
    return pl.pallas_call(
        kern,
        grid=(n // tm,),
        in_specs=[
            pl.BlockSpec((tm, d), rows),
            pl.BlockSpec((None, 1, d), lyr),
            pl.BlockSpec((None, SUBLANES_F32, N_MOD * d), lyr),
            _resident((None, d, IN_WIDTH), lyr),
            pl.BlockSpec((None, 1, HEAD_DIM), lyr),
            pl.BlockSpec((None, 1, HEAD_DIM), lyr),
            pl.BlockSpec((tm, HEAD_DIM), rows),
            pl.BlockSpec((tm, HEAD_DIM), rows),
            pl.BlockSpec((HEAD_DIM, 2 * HEAD_DIM), lambda i: (0, 0)),
        ],
        out_specs=[
            pl.BlockSpec((ATTN_WIDTH, tm), lambda i: (0, i)),
            pl.BlockSpec((tm, KV_WIDTH), rows),
            pl.BlockSpec((N_KV_HEADS * V_ROWS, tm), lambda i: (0, i)),
            pl.BlockSpec((tm, CONV_WIDTH), rows),
            pl.BlockSpec((tm, FOURIER_WIDTH), rows),
            pl.BlockSpec((tm, FOURIER_WIDTH), rows),
        ],
        out_shape=[
            jax.ShapeDtypeStruct((ATTN_WIDTH, n), BF16),
            jax.ShapeDtypeStruct((n, KV_WIDTH), BF16),
            jax.ShapeDtypeStruct((N_KV_HEADS * V_ROWS, n), BF16),
            jax.ShapeDtypeStruct((n, CONV_WIDTH), F32),
            jax.ShapeDtypeStruct((n, FOURIER_WIDTH), BF16),
            jax.ShapeDtypeStruct((n, FOURIER_WIDTH), BF16),
        ],
        compiler_params=_cparams(("parallel",)),
        name="in_projection",
    )(x, norm_g, mods, w_in, q_norm, k_norm, cos, sin, dft_c)


def _attn_kernel(*refs, tk, n_lat_chunks, unroll):
    if n_lat_chunks:
        qt_ref, kc_ref, vtc_ref, kl_ref, vtl_ref, o_ref, acc_sc, m_sc, alpha_sc, p_even, p_odd = refs
    else:
        qt_ref, kc_ref, vtc_ref, o_ref = refs
    heads = range(GROUP)

    def q_t(h):
        return qt_ref[h * HEAD_DIM:(h + 1) * HEAD_DIM, :]

    ms, accs = [], []
    for h in heads:
        s = jnp.dot(kc_ref[...], q_t(h), preferred_element_type=F32)
        m = jnp.max(s, axis=0, keepdims=True)
        p = jnp.exp2(s - m)
        ms.append(m)
        accs.append(jnp.dot(vtc_ref[...], p.astype(BF16), preferred_element_type=F32))

    if n_lat_chunks:
        for h in heads:
            acc_sc[h] = accs[h]
            m_sc[h] = ms[h]
            alpha_sc[h] = jnp.ones_like(ms[h])
        p_bufs = (p_even, p_odd)
        p_odd[...] = jnp.zeros(p_odd.shape, BF16)

        def value_update(c, h, p_src):
            off = pl.multiple_of(c * tk, tk)
            acc_sc[h] = alpha_sc[h] * acc_sc[h] + jnp.dot(vtl_ref[:, pl.ds(off, tk)], p_src[h],
                                                          preferred_element_type=F32)

        def step(c, p_dst, p_src):
            off = pl.multiple_of(c * tk, tk)
            k = kl_ref[pl.ds(off, tk), :]
            ss = [jnp.dot(k, q_t(h), preferred_element_type=F32) for h in heads]
            for h in heads:
                value_update(jnp.maximum(c - 1, 0), h, p_src)
                m_old = m_sc[h]
                m_new = jnp.maximum(m_old, jnp.max(ss[h], axis=0, keepdims=True))
                alpha_sc[h] = jnp.exp2(m_old - m_new)
                m_sc[h] = m_new
                p_dst[h] = jnp.exp2(ss[h] - m_new).astype(BF16)

        def body(i, carry):
            for u in range(unroll):
                step(unroll * i + u, p_bufs[u % 2], p_bufs[(u + 1) % 2])
            return carry

        lax.fori_loop(0, n_lat_chunks // unroll, body, 0)
        for h in heads:
            value_update(n_lat_chunks - 1, h, p_odd)
        accs = [acc_sc[h] for h in heads]
    for h in heads:
        out = accs[h][:HEAD_DIM] / accs[h][HEAD_DIM:HEAD_DIM + 1]
        o_ref[:, h * HEAD_DIM:(h + 1) * HEAD_DIM] = out.T.astype(BF16)


def _attention(qt, k_c, vt_c, k_l=None, vt_l=None):
    n = qt.shape[1]
    nc = k_c.shape[0]
    tq = _pick_tile(n, 512)
    qw = GROUP * HEAD_DIM
    in_specs = [
        pl.BlockSpec((qw, tq), lambda k, i: (k, i)),
        pl.BlockSpec((nc, HEAD_DIM), lambda k, i: (0, k)),
        pl.BlockSpec((V_ROWS, nc), lambda k, i: (k, 0)),
    ]
    args = [qt, k_c, vt_c]
    scratch = []
    tk, n_chunks, unroll = 0, 0, 0
    if k_l is not None:
        nl = k_l.shape[0]
        tk = _pick_tile(nl, 512)
        n_chunks = nl // tk
        unroll = 4 if n_chunks % 4 == 0 else 2
        assert n_chunks % unroll == 0, "latent key chunks are processed in pairs"
        in_specs += [
            pl.BlockSpec((nl, HEAD_DIM), lambda k, i: (0, k)),
            pl.BlockSpec((V_ROWS, nl), lambda k, i: (k, 0)),
        ]
        args += [k_l, vt_l]
        scratch = [pltpu.VMEM((GROUP, V_ROWS, tq), F32),
                   pltpu.VMEM((GROUP, 1, tq), F32), pltpu.VMEM((GROUP, 1, tq), F32),
                   pltpu.VMEM((GROUP, tk, tq), BF16), pltpu.VMEM((GROUP, tk, tq), BF16)]
    return pl.pallas_call(
        functools.partial(_attn_kernel, tk=tk, n_lat_chunks=n_chunks, unroll=unroll),
        grid=(N_KV_HEADS, n // tq),
        in_specs=in_specs,
        out_specs=pl.BlockSpec((tq, qw), lambda k, i: (i, k)),
        out_shape=jax.ShapeDtypeStruct((n, ATTN_WIDTH), BF16),
        scratch_shapes=scratch,
        compiler_params=_cparams(("parallel", "parallel")),
        name="attention",
    )(*args)


def _conv_kernel(yp_ref, yc_ref, yn_ref, dw_ref, g_ref, b_ref, pw_ref, o_ref, ext_sc, act_sc,
                 *, tm, n_tiles, chunk):
    i = pl.program_id(0)
    ext_sc[0, 0:HALO, :] = jnp.where(i > 0, yp_ref[...], 0.0)
    ext_sc[0, HALO:HALO + tm, :] = yc_ref[...]
    ext_sc[0, HALO + tm:2 * HALO + tm, :] = jnp.where(i < n_tiles - 1, yn_ref[...], 0.0)
    n_shift_rows = tm + 2 * HALO - SUBLANES_F32
    for s in range(1, SUBLANES_F32):
        ext_sc[s, 0:n_shift_rows, :] = ext_sc[0, s:s + n_shift_rows, :]

    def body(c, carry):
        r0 = pl.multiple_of(c * chunk, chunk)
        acc = jnp.zeros((chunk, CONV_WIDTH), F32)
        for k in range(CONV_KERNEL):
            off = k + HALO - CONV_PAD
            base = pl.multiple_of(r0 + (off // SUBLANES_F32) * SUBLANES_F32, SUBLANES_F32)
            acc = acc + dw_ref[k:k + 1, :] * ext_sc[off % SUBLANES_F32, pl.ds(base, chunk), :]
        mu = jnp.mean(acc, axis=-1, keepdims=True)
        xc = acc - mu
        var = jnp.mean(xc * xc, axis=-1, keepdims=True)
        yn = xc * lax.rsqrt(var + NORM_EPS) * g_ref[...] + b_ref[...]
        act_sc[pl.ds(r0, chunk), :] = (yn * _sigmoid(yn)).astype(BF16)
        return carry

    lax.fori_loop(0, tm // chunk, body, 0)
    o_ref[...] = jnp.dot(act_sc[...], pw_ref[...], preferred_element_type=F32).astype(BF16)


def _conv_module(y, conv_dw, ln_g, ln_b, conv_pw, *, layer):
    n = y.shape[0]
    tm = _pick_tile(n, 256)
    n_tiles = n // tm
    hb = tm // HALO
    lyr = lambda i: (layer, 0, 0)
    chunk = _pick_tile(tm, 64)
    return pl.pallas_call(
        functools.partial(_conv_kernel, tm=tm, n_tiles=n_tiles, chunk=chunk),
        grid=(n_tiles,),
        in_specs=[
            pl.BlockSpec((HALO, CONV_WIDTH), lambda i: (jnp.maximum(i * hb - 1, 0), 0)),
            pl.BlockSpec((tm, CONV_WIDTH), lambda i: (i, 0)),
            pl.BlockSpec((HALO, CONV_WIDTH), lambda i: (jnp.minimum((i + 1) * hb, n // HALO - 1), 0)),
            pl.BlockSpec((None, CONV_KERNEL, CONV_WIDTH), lyr),
            pl.BlockSpec((None, 1, CONV_WIDTH), lyr),
            pl.BlockSpec((None, 1, CONV_WIDTH), lyr),
            pl.BlockSpec((None, CONV_WIDTH, CONV_WIDTH), lyr),
        ],
        out_specs=pl.BlockSpec((tm, CONV_WIDTH), lambda i: (i, 0)),
        out_shape=jax.ShapeDtypeStruct((n, CONV_WIDTH), BF16),
        scratch_shapes=[
            pltpu.VMEM((SUBLANES_F32, tm + 2 * HALO, CONV_WIDTH), F32),
            pltpu.VMEM((tm, CONV_WIDTH), BF16),
        ],
        compiler_params=_cparams(("parallel",)),
        name="conv_module",
    )(y, y, y, conv_dw, ln_g, ln_b, conv_pw)


def _dft_tables(n_pos):
    f32 = lambda a: jnp.asarray(a.astype(np.float32))
    k = np.arange(HEAD_DIM)
    ang = 2.0 * np.pi * np.outer(k, k) / HEAD_DIM
    chan = np.concatenate([np.cos(ang), -np.sin(ang)], axis=1) / math.sqrt(HEAD_DIM)
    tabs = dict(chan=f32(chan).astype(BF16))
    if n_pos < DFT_N1 * DFT_MIN_N2:
        a = 2.0 * np.pi * np.outer(np.arange(n_pos), np.arange(n_pos)) / n_pos
        dense = np.concatenate([np.cos(a), np.sin(a)], axis=1) / math.sqrt(n_pos)
        tabs.update(dense=f32(dense).astype(BF16))
        return tabs
    n1 = DFT_N1
    n2 = n_pos // n1
    a1 = 2.0 * np.pi * np.outer(np.arange(n1), np.arange(n1)) / n1
    c1, s1 = np.cos(a1) / math.sqrt(n1), np.sin(a1) / math.sqrt(n1)
    m1 = np.block([[c1, s1], [-s1, c1]])
    at = 2.0 * np.pi * np.outer(np.arange(n2), np.arange(n1)) / n_pos
    a2 = 2.0 * np.pi * np.outer(np.arange(n2), np.arange(n2)) / n2
    g2 = np.concatenate([np.cos(a2), np.sin(a2)], axis=1) / math.sqrt(n2)
    twc = jnp.broadcast_to(f32(np.cos(at))[:, :, None], (n2, n1, HEAD_DIM))
    tws = jnp.broadcast_to(f32(np.sin(at))[:, :, None], (n2, n1, HEAD_DIM))
    tabs.update(n1=n1, n2=n2, m1=f32(m1).astype(BF16), twc=twc, tws=tws, g2=f32(g2).astype(BF16))
    return tabs


def _dft_stage1_kernel(zr_ref, zi_ref, m1_ref, twc_ref, tws_ref, br_ref, bi_ref, *, n1, nb):
    for b in range(nb):
        cols = slice(b * FOURIER_WIDTH, (b + 1) * FOURIER_WIDTH)
        z = jnp.concatenate([zr_ref[:, cols], zi_ref[:, cols]], axis=0)
        a = jnp.dot(m1_ref[...], z, preferred_element_type=F32)
        tc = twc_ref[b]
        ts = tws_ref[b]
        for g in range(N_FOURIER_GROUPS):
            lanes = slice(g * HEAD_DIM, (g + 1) * HEAD_DIM)
            ar = a[:n1, lanes]
            ai = a[n1:, lanes]
            br_ref[b, :, lanes] = (ar * tc + ai * ts).astype(BF16)
            bi_ref[b, :, lanes] = (ai * tc - ar * ts).astype(BF16)


def _dft_stage2_kernel(br_ref, bi_ref, g2_ref, y_ref):
    b = jnp.concatenate([br_ref[...], bi_ref[...]], axis=0)
    y_ref[...] = jnp.dot(g2_ref[...], b, preferred_element_type=F32).astype(BF16)


def _dft_dense_kernel(zr_ref, zi_ref, g_ref, y_ref):
    z = jnp.concatenate([zr_ref[...], zi_ref[...]], axis=0)
    y_ref[...] = jnp.dot(g_ref[...], z, preferred_element_type=F32).astype(BF16)


def _position_dft(zr, zi, tabs):
    n, w = zr.shape
    if "dense" in tabs:
        return pl.pallas_call(
            _dft_dense_kernel,
            out_shape=jax.ShapeDtypeStruct((n, w), BF16),
            compiler_params=pltpu.CompilerParams(vmem_limit_bytes=V7X_VMEM_LIMIT_BYTES),
            name="position_dft_dense",
        )(zr, zi, tabs["dense"])
    n1, n2 = tabs["n1"], tabs["n2"]
    nb = _pick_tile(n2, 8)
    zr2 = zr.reshape(n1, n2 * w)
    zi2 = zi.reshape(n1, n2 * w)
    br, bi = pl.pallas_call(
        functools.partial(_dft_stage1_kernel, n1=n1, nb=nb),
        grid=(n2 // nb,),
        in_specs=[
            pl.BlockSpec((n1, nb * w), lambda j: (0, j)),
            pl.BlockSpec((n1, nb * w), lambda j: (0, j)),
            pl.BlockSpec((2 * n1, 2 * n1), lambda j: (0, 0)),
            pl.BlockSpec((nb, n1, HEAD_DIM), lambda j: (j, 0, 0)),
            pl.BlockSpec((nb, n1, HEAD_DIM), lambda j: (j, 0, 0)),
        ],
        out_specs=[pl.BlockSpec((nb, n1, w), lambda j: (j, 0, 0))] * 2,
        out_shape=[jax.ShapeDtypeStruct((n2, n1, w), BF16)] * 2,
        compiler_params=_cparams(("parallel",)),
        name="position_dft_stage1",
    )(zr2, zi2, tabs["m1"], tabs["twc"], tabs["tws"])
    tn = _pick_tile(n1 * w, 8192)
    y = pl.pallas_call(
        _dft_stage2_kernel,
        grid=(n1 * w // tn,),
        in_specs=[
            pl.BlockSpec((n2, tn), lambda j: (0, j)),
            pl.BlockSpec((n2, tn), lambda j: (0, j)),
            pl.BlockSpec((n2, 2 * n2), lambda j: (0, 0)),
        ],
        out_specs=pl.BlockSpec((n2, tn), lambda j: (0, j)),
        out_shape=jax.ShapeDtypeStruct((n2, n1 * w), BF16),
        compiler_params=_cparams(("parallel",)),
        name="position_dft_stage2",
    )(br.reshape(n2, n1 * w), bi.reshape(n2, n1 * w), tabs["g2"])
    return y.reshape(n, w)


def _outproj_kernel(x_ref, att_ref, cv_ref, y_ref, fw_ref, wo_ref, mod_ref, o_ref, *, row, d):
    fo = jnp.dot(y_ref[...], fw_ref[...], preferred_element_type=F32).astype(BF16)
    acc = jnp.dot(att_ref[...], wo_ref[0:ATTN_WIDTH, :], preferred_element_type=F32)
    acc = acc + jnp.dot(cv_ref[...], wo_ref[ATTN_WIDTH:ATTN_WIDTH + CONV_WIDTH, :], preferred_element_type=F32)
    acc = acc + jnp.dot(fo, wo_ref[ATTN_WIDTH + CONV_WIDTH:MIX_WIDTH, :], preferred_element_type=F32)
    gate = mod_ref[row:row + 1, 2 * d:3 * d]
    o_ref[...] = x_ref[...] + gate * acc


def _out_projection(x, att, cv, yf, fourier_w, w_out, mods, *, layer, row):
    n, d = x.shape
    tm = _pick_tile(n, 512)
    lyr = lambda i: (layer, 0, 0)
    rows = lambda i: (i, 0)
    return pl.pallas_call(
        functools.partial(_outproj_kernel, row=row, d=d),
        grid=(n // tm,),
        in_specs=[
            pl.BlockSpec((tm, d), rows),
            pl.BlockSpec((tm, ATTN_WIDTH), rows),
            pl.BlockSpec((tm, CONV_WIDTH), rows),
            pl.BlockSpec((tm, FOURIER_WIDTH), rows),
            _resident((None, FOURIER_WIDTH, FOURIER_WIDTH), lyr),
            _resident((None, MIX_WIDTH, d), lyr),
            pl.BlockSpec((None, SUBLANES_F32, N_MOD * d), lyr),
        ],
        out_specs=pl.BlockSpec((tm, d), rows),
        out_shape=jax.ShapeDtypeStruct((n, d), F32),
        compiler_params=_cparams(("parallel",)),
        name="out_projection",
    )(x, att, cv, yf, fourier_w, w_out, mods)


def _ffn_kernel(xp_ref, x_ref, xn_ref, g_ref, mod_ref, wuv_ref, wug_ref, dwv_ref, dwg_ref, wd_ref,
                o_ref, h_sc, *, row, d, tm, n_tiles, n_hid):
    i = pl.program_id(0)
    j = pl.program_id(1)

    @pl.when(j == 0)
    def _():
        shift = mod_ref[row:row + 1, 3 * d:4 * d]
        scale = mod_ref[row:row + 1, 4 * d:5 * d]

        def norm(x):
            ms = jnp.mean(x * x, axis=-1, keepdims=True)
            return (x * lax.rsqrt(ms + NORM_EPS) * g_ref[...]) * (1.0 + scale) + shift

        h_sc[0:HALO, :] = jnp.where(i > 0, norm(xp_ref[...]), 0.0).astype(BF16)
        h_sc[HALO:HALO + tm, :] = norm(x_ref[...]).astype(BF16)
        h_sc[HALO + tm:2 * HALO + tm, :] = jnp.where(i < n_tiles - 1, norm(xn_ref[...]), 0.0).astype(BF16)

    h = h_sc[...]

    def up_conv(w_ref, dw_ref):
        u = jnp.dot(h, w_ref[...], preferred_element_type=F32)
        return (dw_ref[0:1, :] * u[HALO - 1:HALO - 1 + tm] + dw_ref[1:2, :] * u[HALO:HALO + tm]
                + dw_ref[2:3, :] * u[HALO + 1:HALO + 1 + tm])

    val = up_conv(wuv_ref, dwv_ref)
    gt = up_conv(wug_ref, dwg_ref)
    act = (gt * _sigmoid(gt) * val).astype(BF16)
    contrib = jnp.dot(act, wd_ref[...], preferred_element_type=F32)

    @pl.when(j == 0)
    def _():
        o_ref[...] = contrib

    @pl.when(j > 0)
    def _():
        o_ref[...] += contrib

    @pl.when(j == n_hid - 1)
    def _():
        gate = mod_ref[row:row + 1, 5 * d:6 * d]
        o_ref[...] = x_ref[...] + gate * o_ref[...]


def _conv_ffn(x, norm_g, mods, w_up, ffn_dw, w_down, *, layer, row):
    n, d = x.shape
    hidden = w_down.shape[1]
    tm = _pick_tile(n, 512)
    th = 512 if hidden % 512 == 0 else 256
    n_tiles, n_hid = n // tm, hidden // th
    hb = tm // HALO
    lyr = lambda i, j: (layer, 0, 0)
    return pl.pallas_call(
        functools.partial(_ffn_kernel, row=row, d=d, tm=tm, n_tiles=n_tiles, n_hid=n_hid),
        grid=(n_tiles, n_hid),
        in_specs=[
            pl.BlockSpec((HALO, d), lambda i, j: (jnp.maximum(i * hb - 1, 0), 0)),
            pl.BlockSpec((tm, d), lambda i, j: (i, 0)),
            pl.BlockSpec((HALO, d), lambda i, j: (jnp.minimum((i + 1) * hb, n // HALO - 1), 0)),
            pl.BlockSpec((None, 1, d), lyr),
            pl.BlockSpec((None, SUBLANES_F32, N_MOD * d), lyr),
            pl.BlockSpec((None, d, th), lambda i, j: (layer, 0, j)),
            pl.BlockSpec((None, d, th), lambda i, j: (layer, 0, n_hid + j)),
            pl.BlockSpec((None, FFN_CONV_KERNEL, th), lambda i, j: (layer, 0, j)),
            pl.BlockSpec((None, FFN_CONV_KERNEL, th), lambda i, j: (layer, 0, n_hid + j)),
            pl.BlockSpec((None, th, d), lambda i, j: (layer, j, 0)),
        ],
        out_specs=pl.BlockSpec((tm, d), lambda i, j: (i, 0)),
        out_shape=jax.ShapeDtypeStruct((n, d), F32),
        scratch_shapes=[pltpu.VMEM((tm + 2 * HALO, d), BF16)],
        compiler_params=_cparams(("parallel", "arbitrary")),
        name="conv_ffn",
    )(x, x, x, norm_g, mods, w_up, w_up, ffn_dw, ffn_dw, w_down)


def _final_norm_kernel(x_ref, g_ref, o_ref):
    x = x_ref[...]
    o_ref[...] = x * lax.rsqrt(jnp.mean(x * x, axis=-1, keepdims=True) + NORM_EPS) * g_ref[...]


def _final_norm(x, g):
    n, d = x.shape
    tm = _pick_tile(n, 512)
    return pl.pallas_call(
        _final_norm_kernel,
        grid=(n // tm,),
        in_specs=[pl.BlockSpec((tm, d), lambda i: (i, 0)), pl.BlockSpec((1, d), lambda i: (0, 0))],
        out_specs=pl.BlockSpec((tm, d), lambda i: (i, 0)),
        out_shape=jax.ShapeDtypeStruct((n, d), F32),
        compiler_params=_cparams(("parallel",)),
        name="final_norm",
    )(x, g.reshape(1, d))


def _rope_tables(n_tokens):
    rows = n_tokens // GRID_W
    row = jnp.repeat(jnp.arange(rows), GRID_W).astype(F32)
    col = jnp.tile(jnp.arange(GRID_W), rows).astype(F32)
    n_freq = HEAD_DIM // 4
    inv_freq = ROPE_THETA ** (-jnp.arange(n_freq, dtype=F32) / n_freq)
    ang = jnp.concatenate([row[:, None] * inv_freq, col[:, None] * inv_freq], axis=-1)
    cos, sin = jnp.cos(ang), jnp.sin(ang)
    return jnp.concatenate([cos, cos], axis=-1), jnp.concatenate([-sin, sin], axis=-1)


def kernel(x, c, ctx, c_ctx, w_mod, b_mod, mix_norm, ffn_norm, w_in, q_norm, k_norm, conv_dw, conv_ln_g,
           conv_ln_b, conv_pw, fourier_w, w_out, ffn_up, ffn_dw, ffn_down, final_norm):
    batch, n_lat, d = x.shape
    n_ctx = ctx.shape[1]
    depth = w_mod.shape[0]
    assert batch == 1 and c.shape == (1, d)

    cc = jnp.zeros((SUBLANES_F32, d), F32).at[0].set(c[0]).at[1].set(c_ctx)
    mods = _modulation(cc, w_mod, b_mod)

    w_in_b, w_out_b = w_in.astype(BF16), w_out.astype(BF16)
    ffn_up_b, ffn_down_b = ffn_up.astype(BF16), ffn_down.astype(BF16)
    conv_pw_b, fourier_w_b = conv_pw.astype(BF16), fourier_w.astype(BF16)
    vec = lambda a: a.reshape(depth, 1, a.shape[-1])
    mix_g, ffn_g, qn, kn = vec(mix_norm), vec(ffn_norm), vec(q_norm), vec(k_norm)
    ln_g, ln_b = vec(conv_ln_g), vec(conv_ln_b)

    rope_l = _rope_tables(n_lat)
    rope_c = (jnp.ones((n_ctx, HEAD_DIM), F32), jnp.zeros((n_ctx, HEAD_DIM), F32))
    tabs_l, tabs_c = _dft_tables(n_lat), _dft_tables(n_ctx)

    xl, xc = x[0], ctx[0]
    for l in range(depth):
        last = l == depth - 1
        qtc, kc, vtc, yc, zrc, zic = _in_projection(xc, mix_g, mods, w_in_b, qn, kn, *rope_c, tabs_c["chan"],
                                                    layer=l, row=1)
        qtl, kl, vtl, yl, zrl, zil = _in_projection(xl, mix_g, mods, w_in_b, qn, kn, *rope_l, tabs_l["chan"],
                                                    layer=l, row=0)
        att = _attention(qtl, kc, vtc, kl, vtl)
        cv = _conv_module(yl, conv_dw, ln_g, ln_b, conv_pw_b, layer=l)
        yf = _position_dft(zrl, zil, tabs_l)
        xl = _out_projection(xl, att, cv, yf, fourier_w_b, w_out_b, mods, layer=l, row=0)
        xl = _conv_ffn(xl, ffn_g, mods, ffn_up_b, ffn_dw, ffn_down_b, layer=l, row=0)
        if not last:
            att = _attention(qtc, kc, vtc)
            cv = _conv_module(yc, conv_dw, ln_g, ln_b, conv_pw_b, layer=l)
            yf = _position_dft(zrc, zic, tabs_c)
            xc = _out_projection(xc, att, cv, yf, fourier_w_b, w_out_b, mods, layer=l, row=1)
            xc = _conv_ffn(xc, ffn_g, mods, ffn_up_b, ffn_dw, ffn_down_b, layer=l, row=1)
    return _final_norm(xl, final_norm)[None]
```

```python
import functools
import math

import numpy as np
import jax
import jax.numpy as jnp
from jax import lax
from jax.experimental import pallas as pl
from jax.experimental.pallas import tpu as pltpu

F32 = jnp.float32
BF16 = jnp.bfloat16

HEAD_DIM = 128
N_ATTN_HEADS = 8
N_KV_HEADS = 2
GROUP = N_ATTN_HEADS // N_KV_HEADS
ATTN_WIDTH = N_ATTN_HEADS * HEAD_DIM
KV_WIDTH = N_KV_HEADS * HEAD_DIM
CONV_WIDTH = 4 * HEAD_DIM
FOURIER_WIDTH = 4 * HEAD_DIM
N_FOURIER_GROUPS = 4
Q_END = ATTN_WIDTH
K_END = Q_END + KV_WIDTH
V_END = K_END + KV_WIDTH
GLU_MID = V_END + CONV_WIDTH
CONV_END = V_END + 2 * CONV_WIDTH
IN_WIDTH = CONV_END + FOURIER_WIDTH
MIX_WIDTH = ATTN_WIDTH + CONV_WIDTH + FOURIER_WIDTH
CONV_KERNEL = 31
CONV_PAD = (CONV_KERNEL - 1) // 2
FFN_CONV_KERNEL = 3
GRID_W = 64
ROPE_THETA = 10000.0
NORM_EPS = 1e-6
N_MOD = 6
LOG2E = 1.4426950408889634

V7X_VMEM_LIMIT_BYTES = 56 * 1024 * 1024
SUBLANES_F32 = 8
HALO = 16
DFT_N1 = 128
DFT_MIN_N2 = 16
V_ROWS = HEAD_DIM + 16


def _cparams(sem):
    return pltpu.CompilerParams(dimension_semantics=sem, vmem_limit_bytes=V7X_VMEM_LIMIT_BYTES)


def _resident(shape, index_map):
    return pl.BlockSpec(shape, index_map, pipeline_mode=pl.Buffered(1))


def _sigmoid(x):
    return 1.0 / (1.0 + jnp.exp(-x))


def _pick_tile(n, pref):
    t = min(n, pref)
    while n % t:
        t //= 2
    return t


def _mod_kernel(ct_ref, w_ref, b_ref, o_ref):
    ct = ct_ref[...]
    s = ct * _sigmoid(ct)
    w = w_ref[...]
    rows = [jnp.sum(s[:, r:r + 1] * w, axis=0, keepdims=True) for r in range(2)]
    rows.append(jnp.zeros((SUBLANES_F32 - 2, w.shape[1]), F32))
    o_ref[...] = jnp.concatenate(rows, axis=0) + b_ref[...]


def _modulation(ct, w_mod, b_mod):
    depth, d, n = w_mod.shape
    tn = _pick_tile(n, 1024)
    return pl.pallas_call(
        _mod_kernel,
        grid=(depth, n // tn),
        in_specs=[
            pl.BlockSpec((d, 2), lambda l, j: (0, 0)),
            pl.BlockSpec((None, d, tn), lambda l, j: (l, 0, j)),
            pl.BlockSpec((None, 1, tn), lambda l, j: (l, 0, j)),
        ],
        out_specs=pl.BlockSpec((None, SUBLANES_F32, tn), lambda l, j: (l, 0, j)),
        out_shape=jax.ShapeDtypeStruct((depth, SUBLANES_F32, n), F32),
        compiler_params=_cparams(("arbitrary", "arbitrary")),
        name="adaln_modulation",
    )(ct, w_mod, b_mod.reshape(depth, 1, n))


def _head_norm_rope(z, g, cos, sin):
    y = z * lax.rsqrt(jnp.mean(z * z, axis=-1, keepdims=True) + NORM_EPS) * g
    return y * cos + pltpu.roll(y, HEAD_DIM // 2, axis=1) * sin


def _inproj_kernel(x_ref, g_ref, mod_ref, w_ref, qn_ref, kn_ref, cos_ref, sin_ref, dft_ref,
                   qt_ref, k_ref, vt_ref, y_ref, zr_ref, zi_ref, *, row, d, q_scale):
    x = x_ref[...]
    shift = mod_ref[row:row + 1, 0:d]
    scale = mod_ref[row:row + 1, d:2 * d]
    ms = jnp.mean(x * x, axis=-1, keepdims=True)
    h = (x * lax.rsqrt(ms + NORM_EPS) * g_ref[...]) * (1.0 + scale) + shift
    hb = h.astype(BF16)
    cos = cos_ref[...]
    sin = sin_ref[...]

    def proj(lo, hi):
        return jnp.dot(hb, w_ref[:, lo:hi], preferred_element_type=F32)

    for p in range(N_ATTN_HEADS // 2):
        z = proj(2 * p * HEAD_DIM, (2 * p + 2) * HEAD_DIM)
        for j in range(2):
            hd = 2 * p + j
            qh = _head_norm_rope(z[:, j * HEAD_DIM:(j + 1) * HEAD_DIM], qn_ref[...], cos, sin)
            qt_ref[hd * HEAD_DIM:(hd + 1) * HEAD_DIM, :] = (qh * q_scale).T.astype(BF16)
    z = proj(Q_END, K_END)
    for j in range(N_KV_HEADS):
        kh = _head_norm_rope(z[:, j * HEAD_DIM:(j + 1) * HEAD_DIM], kn_ref[...], cos, sin)
        k_ref[:, j * HEAD_DIM:(j + 1) * HEAD_DIM] = kh.astype(BF16)
    vt = proj(K_END, V_END).T
    pad_rows = V_ROWS - HEAD_DIM
    ones_row = (lax.broadcasted_iota(jnp.int32, (pad_rows, vt.shape[1]), 0) == 0).astype(BF16)
    for j in range(N_KV_HEADS):
        vt_ref[j * V_ROWS:j * V_ROWS + HEAD_DIM, :] = vt[j * HEAD_DIM:(j + 1) * HEAD_DIM].astype(BF16)
        vt_ref[j * V_ROWS + HEAD_DIM:(j + 1) * V_ROWS, :] = ones_row
    y_ref[...] = proj(V_END, GLU_MID) * _sigmoid(proj(GLU_MID, CONV_END))
    f = proj(CONV_END, IN_WIDTH)
    for g in range(N_FOURIER_GROUPS):
        r = jnp.dot(f[:, g * HEAD_DIM:(g + 1) * HEAD_DIM].astype(BF16), dft_ref[...],
                    preferred_element_type=F32)
        zr_ref[:, g * HEAD_DIM:(g + 1) * HEAD_DIM] = r[:, :HEAD_DIM].astype(BF16)
        zi_ref[:, g * HEAD_DIM:(g + 1) * HEAD_DIM] = r[:, HEAD_DIM:].astype(BF16)


def _in_projection(x, norm_g, mods, w_in, q_norm, k_norm, cos, sin, dft_c, *, layer, row):
    n, d = x.shape
    tm = _pick_tile(n, 512)
    lyr = lambda i: (layer, 0, 0)
    rows = lambda i: (i, 0)
    kern = functools.partial(_inproj_kernel, row=row, d=d, q_scale=HEAD_DIM ** -0.5 * LOG2E)
    return pl.pallas_call(
        kern,
        grid=(n // tm,),
        in_specs=[
            pl.BlockSpec((tm, d), rows),
            pl.BlockSpec((None, 1, d), lyr),
            pl.BlockSpec((None, SUBLANES_F32, N_MOD * d), lyr),
            _resident((None, d, IN_WIDTH), lyr),
            pl.BlockSpec((None, 1, HEAD_DIM), lyr),
            pl.BlockSpec((None, 1, HEAD_DIM), lyr),
            pl.BlockSpec((tm, HEAD_DIM), rows),
            pl.BlockSpec((tm, HEAD_DIM), rows),
            pl.BlockSpec((HEAD_DIM, 2 * HEAD_DIM), lambda i: (0, 0)),
        ],
        out_specs=[
            pl.BlockSpec((ATTN_WIDTH, tm), lambda i: (0, i)),
            pl.BlockSpec((tm, KV_WIDTH), rows),
            pl.BlockSpec((N_KV_HEADS * V_ROWS, tm), lambda i: (0, i)),
            pl.BlockSpec((tm, CONV_WIDTH), rows),
            pl.BlockSpec((tm, FOURIER_WIDTH), rows),
            pl.BlockSpec((tm, FOURIER_WIDTH), rows),
        ],
        out_shape=[
            jax.ShapeDtypeStruct((ATTN_WIDTH, n), BF16),
            jax.ShapeDtypeStruct((n, KV_WIDTH), BF16),
            jax.ShapeDtypeStruct((N_KV_HEADS * V_ROWS, n), BF16),
            jax.ShapeDtypeStruct((n, CONV_WIDTH), F32),
            jax.ShapeDtypeStruct((n, FOURIER_WIDTH), BF16),
            jax.ShapeDtypeStruct((n, FOURIER_WIDTH), BF16),
        ],
        compiler_params=_cparams(("parallel",)),
        name="in_projection",
    )(x, norm_g, mods, w_in, q_norm, k_norm, cos, sin, dft_c)


def _attn_kernel(*refs, tk, n_lat_chunks, unroll):
    if n_lat_chunks:
        qt_ref, kc_ref, vtc_ref, kl_ref, vtl_ref, o_ref, acc_sc, m_sc, alpha_sc, p_even, p_odd = refs
    else:
        qt_ref, kc_ref, vtc_ref, o_ref = refs
    heads = range(GROUP)

    def q_t(h):
        return qt_ref[h * HEAD_DIM:(h + 1) * HEAD_DIM, :]

    ms, accs = [], []
    for h in heads:
        s = jnp.dot(kc_ref[...], q_t(h), preferred_element_type=F32)
        m = jnp.max(s, axis=0, keepdims=True)
        p = jnp.exp2(s - m)
        ms.append(m)
        accs.append(jnp.dot(vtc_ref[...], p.astype(BF16), preferred_element_type=F32))

    if n_lat_chunks:
        for h in heads:
            acc_sc[h] = accs[h]
            m_sc[h] = ms[h]
            alpha_sc[h] = jnp.ones_like(ms[h])
        p_bufs = (p_even, p_odd)
        p_odd[...] = jnp.zeros(p_odd.shape, BF16)

        def value_update(c, h, p_src):
            off = pl.multiple_of(c * tk, tk)
            acc_sc[h] = alpha_sc[h] * acc_sc[h] + jnp.dot(vtl_ref[:, pl.ds(off, tk)], p_src[h],
                                                          preferred_element_type=F32)

        def step(c, p_dst, p_src):
            off = pl.multiple_of(c * tk, tk)
            k = kl_ref[pl.ds(off, tk), :]
            ss = [jnp.dot(k, q_t(h), preferred_element_type=F32) for h in heads]
            for h in heads:
                value_update(jnp.maximum(c - 1, 0), h, p_src)
                m_old = m_sc[h]
                m_new = jnp.maximum(m_old, jnp.max(ss[h], axis=0, keepdims=True))
                alpha_sc[h] = jnp.exp2(m_old - m_new)
                m_sc[h] = m_new
                p_dst[h] = jnp.exp2(ss[h] - m_new).astype(BF16)

        def body(i, carry):
            for u in range(unroll):
                step(unroll * i + u, p_bufs[u % 2], p_bufs[(u + 1) % 2])
            return carry

        lax.fori_loop(0, n_lat_chunks // unroll, body, 0)
        for h in heads:
            value_update(n_lat_chunks - 1, h, p_odd)
        accs = [acc_sc[h] for h in heads]
    for h in heads:
        out = accs[h][:HEAD_DIM] / accs[h][HEAD_DIM:HEAD_DIM + 1]
        o_ref[:, h * HEAD_DIM:(h + 1) * HEAD_DIM] = out.T.astype(BF16)


def _attention(qt, k_c, vt_c, k_l=None, vt_l=None):
    n = qt.shape[1]
    nc = k_c.shape[0]
    tq = _pick_tile(n, 512)
    qw = GROUP * HEAD_DIM
    in_specs = [
        pl.BlockSpec((qw, tq), lambda k, i: (k, i)),
        pl.BlockSpec((nc, HEAD_DIM), lambda k, i: (0, k)),
        pl.BlockSpec((V_ROWS, nc), lambda k, i: (k, 0)),
    ]
    args = [qt, k_c, vt_c]
    scratch = []
    tk, n_chunks, unroll = 0, 0, 0
    if k_l is not None:
        nl = k_l.shape[0]
        tk = _pick_tile(nl, 512)
        n_chunks = nl // tk
        unroll = 4 if n_chunks % 4 == 0 else 2
        assert n_chunks % unroll == 0, "latent key chunks are processed in pairs"
        in_specs += [
            pl.BlockSpec((nl, HEAD_DIM), lambda k, i: (0, k)),
            pl.BlockSpec((V_ROWS, nl), lambda k, i: (k, 0)),
        ]
        args += [k_l, vt_l]
        scratch = [pltpu.VMEM((GROUP, V_ROWS, tq), F32),
                   pltpu.VMEM((GROUP, 1, tq), F32), pltpu.VMEM((GROUP, 1, tq), F32),
                   pltpu.VMEM((GROUP, tk, tq), BF16), pltpu.VMEM((GROUP, tk, tq), BF16)]
    return pl.pallas_call(
        functools.partial(_attn_kernel, tk=tk, n_lat_chunks=n_chunks, unroll=unroll),
        grid=(N_KV_HEADS, n // tq),
        in_specs=in_specs,
        out_specs=pl.BlockSpec((tq, qw), lambda k, i: (i, k)),
        out_shape=jax.ShapeDtypeStruct((n, ATTN_WIDTH), BF16),
        scratch_shapes=scratch,
        compiler_params=_cparams(("parallel", "parallel")),
        name="attention",
    )(*args)


def _conv_kernel(yp_ref, yc_ref, yn_ref, dw_ref, g_ref, b_ref, pw_ref, o_ref, ext_sc, act_sc,
                 *, tm, n_tiles, chunk):
    i = pl.program_id(0)
    ext_sc[0, 0:HALO, :] = jnp.where(i > 0, yp_ref[...], 0.0)
    ext_sc[0, HALO:HALO + tm, :] = yc_ref[...]
    ext_sc[0, HALO + tm:2 * HALO + tm, :] = jnp.where(i < n_tiles - 1, yn_ref[...], 0.0)
    n_shift_rows = tm + 2 * HALO - SUBLANES_F32
    for s in range(1, SUBLANES_F32):
        ext_sc[s, 0:n_shift_rows, :] = ext_sc[0, s:s + n_shift_rows, :]

    def body(c, carry):
        r0 = pl.multiple_of(c * chunk, chunk)
        acc = jnp.zeros((chunk, CONV_WIDTH), F32)
        for k in range(CONV_KERNEL):
            off = k + HALO - CONV_PAD
            base = pl.multiple_of(r0 + (off // SUBLANES_F32) * SUBLANES_F32, SUBLANES_F32)
            acc = acc + dw_ref[k:k + 1, :] * ext_sc[off % SUBLANES_F32, pl.ds(base, chunk), :]
        mu = jnp.mean(acc, axis=-1, keepdims=True)
        xc = acc - mu
        var = jnp.mean(xc * xc, axis=-1, keepdims=True)
        yn = xc * lax.rsqrt(var + NORM_EPS) * g_ref[...] + b_ref[...]
        act_sc[pl.ds(r0, chunk), :] = (yn * _sigmoid(yn)).astype(BF16)
        return carry

    lax.fori_loop(0, tm // chunk, body, 0)
    o_ref[...] = jnp.dot(act_sc[...], pw_ref[...], preferred_element_type=F32).astype(BF16)


def _conv_module(y, conv_dw, ln_g, ln_b, conv_pw, *, layer):
    n = y.shape[0]
    tm = _pick_tile(n, 256)
    n_tiles = n // tm
    hb = tm // HALO
    lyr = lambda i: (layer, 0, 0)
    chunk = _pick_tile(tm, 64)
    return pl.pallas_call(
        functools.partial(_conv_kernel, tm=tm, n_tiles=n_tiles, chunk=chunk),
        grid=(n_tiles,),
        in_specs=[
            pl.BlockSpec((HALO, CONV_WIDTH), lambda i: (jnp.maximum(i * hb - 1, 0), 0)),
            pl.BlockSpec((tm, CONV_WIDTH), lambda i: (i, 0)),
            pl.BlockSpec((HALO, CONV_WIDTH), lambda i: (jnp.minimum((i + 1) * hb, n // HALO - 1), 0)),
            pl.BlockSpec((None, CONV_KERNEL, CONV_WIDTH), lyr),
            pl.BlockSpec((None, 1, CONV_WIDTH), lyr),
            pl.BlockSpec((None, 1, CONV_WIDTH), lyr),
            pl.BlockSpec((None, CONV_WIDTH, CONV_WIDTH), lyr),
        ],
        out_specs=pl.BlockSpec((tm, CONV_WIDTH), lambda i: (i, 0)),
        out_shape=jax.ShapeDtypeStruct((n, CONV_WIDTH), BF16),
        scratch_shapes=[
            pltpu.VMEM((SUBLANES_F32, tm + 2 * HALO, CONV_WIDTH), F32),
            pltpu.VMEM((tm, CONV_WIDTH), BF16),
        ],
        compiler_params=_cparams(("parallel",)),
        name="conv_module",
    )(y, y, y, conv_dw, ln_g, ln_b, conv_pw)


def _dft_tables(n_pos):
    f32 = lambda a: jnp.asarray(a.astype(np.float32))
    k = np.arange(HEAD_DIM)
    ang = 2.0 * np.pi * np.outer(k, k) / HEAD_DIM
    chan = np.concatenate([np.cos(ang), -np.sin(ang)], axis=1) / math.sqrt(HEAD_DIM)
    tabs = dict(chan=f32(chan).astype(BF16))
    if n_pos < DFT_N1 * DFT_MIN_N2:
        a = 2.0 * np.pi * np.outer(np.arange(n_pos), np.arange(n_pos)) / n_pos
        dense = np.concatenate([np.cos(a), np.sin(a)], axis=1) / math.sqrt(n_pos)
        tabs.update(dense=f32(dense).astype(BF16))
        return tabs
    n1 = DFT_N1
    n2 = n_pos // n1
    a1 = 2.0 * np.pi * np.outer(np.arange(n1), np.arange(n1)) / n1
    c1, s1 = np.cos(a1) / math.sqrt(n1), np.sin(a1) / math.sqrt(n1)
    m1 = np.block([[c1, s1], [-s1, c1]])
    at = 2.0 * np.pi * np.outer(np.arange(n2), np.arange(n1)) / n_pos
    a2 = 2.0 * np.pi * np.outer(np.arange(n2), np.arange(n2)) / n2
    g2 = np.concatenate([np.cos(a2), np.sin(a2)], axis=1) / math.sqrt(n2)
    twc = jnp.broadcast_to(f32(np.cos(at))[:, :, None], (n2, n1, HEAD_DIM))
    tws = jnp.broadcast_to(f32(np.sin(at))[:, :, None], (n2, n1, HEAD_DIM))
    tabs.update(n1=n1, n2=n2, m1=f32(m1).astype(BF16), twc=twc, tws=tws, g2=f32(g2).astype(BF16))
    return tabs


def _dft_stage1_kernel(zr_ref, zi_ref, m1_ref, twc_ref, tws_ref, br_ref, bi_ref, *, n1, nb):
    for b in range(nb):
        cols = slice(b * FOURIER_WIDTH, (b + 1) * FOURIER_WIDTH)
        z = jnp.concatenate([zr_ref[:, cols], zi_ref[:, cols]], axis=0)
        a = jnp.dot(m1_ref[...], z, preferred_element_type=F32)
        tc = twc_ref[b]
        ts = tws_ref[b]
        for g in range(N_FOURIER_GROUPS):
            lanes = slice(g * HEAD_DIM, (g + 1) * HEAD_DIM)
            ar = a[:n1, lanes]
            ai = a[n1:, lanes]
            br_ref[b, :, lanes] = (ar * tc + ai * ts).astype(BF16)
            bi_ref[b, :, lanes] = (ai * tc - ar * ts).astype(BF16)


def _dft_stage2_kernel(br_ref, bi_ref, g2_ref, y_ref):
    b = jnp.concatenate([br_ref[...], bi_ref[...]], axis=0)
    y_ref[...] = jnp.dot(g2_ref[...], b, preferred_element_type=F32).astype(BF16)


def _dft_dense_kernel(zr_ref, zi_ref, g_ref, y_ref):
    z = jnp.concatenate([zr_ref[...], zi_ref[...]], axis=0)
    y_ref[...] = jnp.dot(g_ref[...], z, preferred_element_type=F32).astype(BF16)


def _position_dft(zr, zi, tabs):
    n, w = zr.shape
    if "dense" in tabs:
        return pl.pallas_call(
            _dft_dense_kernel,
            out_shape=jax.ShapeDtypeStruct((n, w), BF16),
            compiler_params=pltpu.CompilerParams(vmem_limit_bytes=V7X_VMEM_LIMIT_BYTES),
            name="position_dft_dense",
        )(zr, zi, tabs["dense"])
    n1, n2 = tabs["n1"], tabs["n2"]
    nb = _pick_tile(n2, 8)
    zr2 = zr.reshape(n1, n2 * w)
    zi2 = zi.reshape(n1, n2 * w)
    br, bi = pl.pallas_call(
        functools.partial(_dft_stage1_kernel, n1=n1, nb=nb),
        grid=(n2 // nb,),
        in_specs=[
            pl.BlockSpec((n1, nb * w), lambda j: (0, j)),
            pl.BlockSpec((n1, nb * w), lambda j: (0, j)),
            pl.BlockSpec((2 * n1, 2 * n1), lambda j: (0, 0)),
            pl.BlockSpec((nb, n1, HEAD_DIM), lambda j: (j, 0, 0)),
            pl.BlockSpec((nb, n1, HEAD_DIM), lambda j: (j, 0, 0)),
        ],
        out_specs=[pl.BlockSpec((nb, n1, w), lambda j: (j, 0, 0))] * 2,
        out_shape=[jax.ShapeDtypeStruct((n2, n1, w), BF16)] * 2,
        compiler_params=_cparams(("parallel",)),
        name="position_dft_stage1",
    )(zr2, zi2, tabs["m1"], tabs["twc"], tabs["tws"])
    tn = _pick_tile(n1 * w, 8192)
    y = pl.pallas_call(
        _dft_stage2_kernel,
        grid=(n1 * w // tn,),
        in_specs=[
            pl.BlockSpec((n2, tn), lambda j: (0, j)),
            pl.BlockSpec((n2, tn), lambda j: (0, j)),
            pl.BlockSpec((n2, 2 * n2), lambda j: (0, 0)),
        ],
        out_specs=pl.BlockSpec((n2, tn), lambda j: (0, j)),
        out_shape=jax.ShapeDtypeStruct((n2, n1 * w), BF16),
        compiler_params=_cparams(("parallel",)),
        name="position_dft_stage2",
    )(br.reshape(n2, n1 * w), bi.reshape(n2, n1 * w), tabs["g2"])
    return y.reshape(n, w)


def _outproj_kernel(x_ref, att_ref, cv_ref, y_ref, fw_ref, wo_ref, mod_ref, o_ref, *, row, d):
    fo = jnp.dot(y_ref[...], fw_ref[...], preferred_element_type=F32).astype(BF16)
    acc = jnp.dot(att_ref[...], wo_ref[0:ATTN_WIDTH, :], preferred_element_type=F32)
    acc = acc + jnp.dot(cv_ref[...], wo_ref[ATTN_WIDTH:ATTN_WIDTH + CONV_WIDTH, :], preferred_element_type=F32)
    acc = acc + jnp.dot(fo, wo_ref[ATTN_WIDTH + CONV_WIDTH:MIX_WIDTH, :], preferred_element_type=F32)
    gate = mod_ref[row:row + 1, 2 * d:3 * d]
    o_ref[...] = x_ref[...] + gate * acc


def _out_projection(x, att, cv, yf, fourier_w, w_out, mods, *, layer, row):
    n, d = x.shape
    tm = _pick_tile(n, 512)
    lyr = lambda i: (layer, 0, 0)
    rows = lambda i: (i, 0)
    return pl.pallas_call(
        functools.partial(_outproj_kernel, row=row, d=d),
        grid=(n // tm,),
        in_specs=[
            pl.BlockSpec((tm, d), rows),
            pl.BlockSpec((tm, ATTN_WIDTH), rows),
            pl.BlockSpec((tm, CONV_WIDTH), rows),
            pl.BlockSpec((tm, FOURIER_WIDTH), rows),
            _resident((None, FOURIER_WIDTH, FOURIER_WIDTH), lyr),
            _resident((None, MIX_WIDTH, d), lyr),
            pl.BlockSpec((None, SUBLANES_F32, N_MOD * d), lyr),
        ],
        out_specs=pl.BlockSpec((tm, d), rows),
        out_shape=jax.ShapeDtypeStruct((n, d), F32),
        compiler_params=_cparams(("parallel",)),
        name="out_projection",
    )(x, att, cv, yf, fourier_w, w_out, mods)


def _ffn_kernel(xp_ref, x_ref, xn_ref, g_ref, mod_ref, wuv_ref, wug_ref, dwv_ref, dwg_ref, wd_ref,
                o_ref, h_sc, act_sc, *, row, d, tm, n_tiles, n_hid):
    i = pl.program_id(0)
    j = pl.program_id(1)

    def up_proj():
        h = h_sc[...]
        uv = jnp.dot(h, wuv_ref[...], preferred_element_type=F32)
        ug = jnp.dot(h, wug_ref[...], preferred_element_type=F32)
        return uv, ug

    def conv_gate(us, slot):
        def conv(u, dw_ref):
            return (dw_ref[0:1, :] * u[HALO - 1:HALO - 1 + tm] + dw_ref[1:2, :] * u[HALO:HALO + tm]
                    + dw_ref[2:3, :] * u[HALO + 1:HALO + 1 + tm])

        val = conv(us[0], dwv_ref)
        gt = conv(us[1], dwg_ref)
        act_sc[slot] = (gt * _sigmoid(gt) * val).astype(BF16)

    def down_proj(slot):
        return jnp.dot(act_sc[slot], wd_ref[...], preferred_element_type=F32)

    @pl.when(j == 0)
    def _():
        shift = mod_ref[row:row + 1, 3 * d:4 * d]
        scale = mod_ref[row:row + 1, 4 * d:5 * d]

        def norm(x):
            ms = jnp.mean(x * x, axis=-1, keepdims=True)
            return (x * lax.rsqrt(ms + NORM_EPS) * g_ref[...]) * (1.0 + scale) + shift

        h_sc[0:HALO, :] = jnp.where(i > 0, norm(xp_ref[...]), 0.0).astype(BF16)
        h_sc[HALO:HALO + tm, :] = norm(x_ref[...]).astype(BF16)
        h_sc[HALO + tm:2 * HALO + tm, :] = jnp.where(i < n_tiles - 1, norm(xn_ref[...]), 0.0).astype(BF16)
        o_ref[...] = jnp.zeros(o_ref.shape, F32)
        conv_gate(up_proj(), 0)

    for parity in range(2):
        @pl.when((j > 0) & (j < n_hid) & (j % 2 == parity))
        def _(parity=parity):
            us = up_proj()
            o_ref[...] += down_proj(1 - parity)
            conv_gate(us, parity)

    @pl.when(j == n_hid)
    def _():
        gate = mod_ref[row:row + 1, 5 * d:6 * d]
        o_ref[...] = x_ref[...] + gate * (o_ref[...] + down_proj((n_hid - 1) % 2))


def _conv_ffn(x, norm_g, mods, w_up, ffn_dw, w_down, *, layer, row):
    n, d = x.shape
    hidden = w_down.shape[1]
    tm = _pick_tile(n, 1024)
    th = 512 if hidden % 512 == 0 else 256
    n_tiles, n_hid = n // tm, hidden // th
    hb = tm // HALO
    lyr = lambda i, j: (layer, 0, 0)
    up_tile = lambda j: jnp.minimum(j, n_hid - 1)
    down_tile = lambda j: jnp.maximum(j - 1, 0)
    return pl.pallas_call(
        functools.partial(_ffn_kernel, row=row, d=d, tm=tm, n_tiles=n_tiles, n_hid=n_hid),
        grid=(n_tiles, n_hid + 1),
        in_specs=[
            pl.BlockSpec((HALO, d), lambda i, j: (jnp.maximum(i * hb - 1, 0), 0)),
            pl.BlockSpec((tm, d), lambda i, j: (i, 0), pipeline_mode=pl.Buffered(1)),
            pl.BlockSpec((HALO, d), lambda i, j: (jnp.minimum((i + 1) * hb, n // HALO - 1), 0)),
            pl.BlockSpec((None, 1, d), lyr),
            pl.BlockSpec((None, SUBLANES_F32, N_MOD * d), lyr),
            pl.BlockSpec((None, d, th), lambda i, j: (layer, 0, up_tile(j))),
            pl.BlockSpec((None, d, th), lambda i, j: (layer, 0, n_hid + up_tile(j))),
            pl.BlockSpec((None, FFN_CONV_KERNEL, th), lambda i, j: (layer, 0, up_tile(j))),
            pl.BlockSpec((None, FFN_CONV_KERNEL, th), lambda i, j: (layer, 0, n_hid + up_tile(j))),
            pl.BlockSpec((None, th, d), lambda i, j: (layer, down_tile(j), 0)),
        ],
        out_specs=pl.BlockSpec((tm, d), lambda i, j: (i, 0)),
        out_shape=jax.ShapeDtypeStruct((n, d), F32),
        scratch_shapes=[pltpu.VMEM((tm + 2 * HALO, d), BF16), pltpu.VMEM((2, tm, th), BF16)],
        compiler_params=_cparams(("parallel", "arbitrary")),
        name="conv_ffn",
    )(x, x, x, norm_g, mods, w_up, w_up, ffn_dw, ffn_dw, w_down)


def _final_norm_kernel(x_ref, g_ref, o_ref):
    x = x_ref[...]
    o_ref[...] = x * lax.rsqrt(jnp.mean(x * x, axis=-1, keepdims=True) + NORM_EPS) * g_ref[...]


def _final_norm(x, g):
    n, d = x.shape
    tm = _pick_tile(n, 512)
    return pl.pallas_call(
        _final_norm_kernel,
        grid=(n // tm,),
        in_specs=[pl.BlockSpec((tm, d), lambda i: (i, 0)), pl.BlockSpec((1, d), lambda i: (0, 0))],
        out_specs=pl.BlockSpec((tm, d), lambda i: (i, 0)),
        out_shape=jax.ShapeDtypeStruct((n, d), F32),
        compiler_params=_cparams(("parallel",)),
        name="final_norm",
    )(x, g.reshape(1, d))


def _rope_tables(n_tokens):
    rows = n_tokens // GRID_W
    row = jnp.repeat(jnp.arange(rows), GRID_W).astype(F32)
    col = jnp.tile(jnp.arange(GRID_W), rows).astype(F32)
    n_freq = HEAD_DIM // 4
    inv_freq = ROPE_THETA ** (-jnp.arange(n_freq, dtype=F32) / n_freq)
    ang = jnp.concatenate([row[:, None] * inv_freq, col[:, None] * inv_freq], axis=-1)
    cos, sin = jnp.cos(ang), jnp.sin(ang)
    return jnp.concatenate([cos, cos], axis=-1), jnp.concatenate([-sin, sin], axis=-1)


def kernel(x, c, ctx, c_ctx, w_mod, b_mod, mix_norm, ffn_norm, w_in, q_norm, k_norm, conv_dw, conv_ln_g,
           conv_ln_b, conv_pw, fourier_w, w_out, ffn_up, ffn_dw, ffn_down, final_norm):
    batch, n_lat, d = x.shape
    n_ctx = ctx.shape[1]
    depth = w_mod.shape[0]
    assert batch == 1 and c.shape == (1, d)

    ct = jnp.stack([c[0], c_ctx], axis=1)
    mods = _modulation(ct, w_mod, b_mod)

    w_in_b, w_out_b = w_in.astype(BF16), w_out.astype(BF16)
    ffn_up_b, ffn_down_b = ffn_up.astype(BF16), ffn_down.astype(BF16)
    conv_pw_b, fourier_w_b = conv_pw.astype(BF16), fourier_w.astype(BF16)
    vec = lambda a: a.reshape(depth, 1, a.shape[-1])
    mix_g, ffn_g, qn, kn = vec(mix_norm), vec(ffn_norm), vec(q_norm), vec(k_norm)
    ln_g, ln_b = vec(conv_ln_g), vec(conv_ln_b)

    rope_l = _rope_tables(n_lat)
    rope_c = (jnp.ones((n_ctx, HEAD_DIM), F32), jnp.zeros((n_ctx, HEAD_DIM), F32))
    tabs_l, tabs_c = _dft_tables(n_lat), _dft_tables(n_ctx)

    xl, xc = x[0], ctx[0]
    for l in range(depth):
        last = l == depth - 1
        qtc, kc, vtc, yc, zrc, zic = _in_projection(xc, mix_g, mods, w_in_b, qn, kn, *rope_c, tabs_c["chan"],
                                                    layer=l, row=1)
        qtl, kl, vtl, yl, zrl, zil = _in_projection(xl, mix_g, mods, w_in_b, qn, kn, *rope_l, tabs_l["chan"],
                                                    layer=l, row=0)
        att = _attention(qtl, kc, vtc, kl, vtl)
        cv = _conv_module(yl, conv_dw, ln_g, ln_b, conv_pw_b, layer=l)
        yf = _position_dft(zrl, zil, tabs_l)
        xl = _out_projection(xl, att, cv, yf, fourier_w_b, w_out_b, mods, layer=l, row=0)
        xl = _conv_ffn(xl, ffn_g, mods, ffn_up_b, ffn_dw, ffn_down_b, layer=l, row=0)
        if not last:
            att = _attention(qtc, kc, vtc)
            cv = _conv_module(yc, conv_dw, ln_g, ln_b, conv_pw_b, layer=l)
            yf = _position_dft(zrc, zic, tabs_c)
            xc = _out_projection(xc, att, cv, yf, fourier_w_b, w_out_b, mods, layer=l, row=1)
            xc = _conv_ffn(xc, ffn_g, mods, ffn_up_b, ffn_dw, ffn_down_b, layer=l, row=1)
    return _final_norm(xl, final_norm)[None]
```

```python
import functools
import math

import numpy as np
import jax
import jax.numpy as jnp
from jax import lax
from jax.experimental import pallas as pl
from jax.experimental.pallas import tpu as pltpu

F32 = jnp.float32
BF16 = jnp.bfloat16

HEAD_DIM = 128
N_ATTN_HEADS = 8
N_KV_HEADS = 2
GROUP = N_ATTN_HEADS // N_KV_HEADS
ATTN_WIDTH = N_ATTN_HEADS * HEAD_DIM
KV_WIDTH = N_KV_HEADS * HEAD_DIM
CONV_WIDTH = 4 * HEAD_DIM
FOURIER_WIDTH = 4 * HEAD_DIM
N_FOURIER_GROUPS = 4
Q_END = ATTN_WIDTH
K_END = Q_END + KV_WIDTH
V_END = K_END + KV_WIDTH
GLU_MID = V_END + CONV_WIDTH
CONV_END = V_END + 2 * CONV_WIDTH
IN_WIDTH = CONV_END + FOURIER_WIDTH
MIX_WIDTH = ATTN_WIDTH + CONV_WIDTH + FOURIER_WIDTH
CONV_KERNEL = 31
CONV_PAD = (CONV_KERNEL - 1) // 2
FFN_CONV_KERNEL = 3
GRID_W = 64
ROPE_THETA = 10000.0
NORM_EPS = 1e-6
N_MOD = 6
LOG2E = 1.4426950408889634

V7X_VMEM_LIMIT_BYTES = 56 * 1024 * 1024
SUBLANES_F32 = 8
HALO = 16
DFT_N1 = 128
DFT_MIN_N2 = 16
V_ROWS = HEAD_DIM + 16


def _cparams(sem):
    return pltpu.CompilerParams(dimension_semantics=sem, vmem_limit_bytes=V7X_VMEM_LIMIT_BYTES)


def _resident(shape, index_map):
    return pl.BlockSpec(shape, index_map, pipeline_mode=pl.Buffered(1))


def _sigmoid(x):
    return 1.0 / (1.0 + jnp.exp(-x))


def _pick_tile(n, pref):
    t = min(n, pref)
    while n % t:
        t //= 2
    return t


def _mod_kernel(ct_ref, w_ref, b_ref, o_ref):
    ct = ct_ref[...]
    s = ct * _sigmoid(ct)
    w = w_ref[...]
    rows = [jnp.sum(s[:, r:r + 1] * w, axis=0, keepdims=True) for r in range(2)]
    rows.append(jnp.zeros((SUBLANES_F32 - 2, w.shape[1]), F32))
    o_ref[...] = jnp.concatenate(rows, axis=0) + b_ref[...]


def _modulation(ct, w_mod, b_mod):
    depth, d, n = w_mod.shape
    tn = _pick_tile(n, 1024)
    return pl.pallas_call(
        _mod_kernel,
        grid=(depth, n // tn),
        in_specs=[
            pl.BlockSpec((d, 2), lambda l, j: (0, 0)),
            pl.BlockSpec((None, d, tn), lambda l, j: (l, 0, j)),
            pl.BlockSpec((None, 1, tn), lambda l, j: (l, 0, j)),
        ],
        out_specs=pl.BlockSpec((None, SUBLANES_F32, tn), lambda l, j: (l, 0, j)),
        out_shape=jax.ShapeDtypeStruct((depth, SUBLANES_F32, n), F32),
        compiler_params=_cparams(("arbitrary", "arbitrary")),
        name="adaln_modulation",
    )(ct, w_mod, b_mod.reshape(depth, 1, n))


def _head_norm_rope(z, g, cos, sin):
    y = z * lax.rsqrt(jnp.mean(z * z, axis=-1, keepdims=True) + NORM_EPS) * g
    return y * cos + pltpu.roll(y, HEAD_DIM // 2, axis=1) * sin


def _inproj_kernel(x_ref, g_ref, mod_ref, w_ref, qn_ref, kn_ref, cos_ref, sin_ref, dft_ref,
                   qt_ref, k_ref, vt_ref, y_ref, zr_ref, zi_ref, *, row, d, q_scale):
    x = x_ref[...]
    shift = mod_ref[row:row + 1, 0:d]
    scale = mod_ref[row:row + 1, d:2 * d]
    ms = jnp.mean(x * x, axis=-1, keepdims=True)
    h = (x * lax.rsqrt(ms + NORM_EPS) * g_ref[...]) * (1.0 + scale) + shift
    hb = h.astype(BF16)
    cos = cos_ref[...]
    sin = sin_ref[...]

    def proj(lo, hi):
        return jnp.dot(hb, w_ref[:, lo:hi], preferred_element_type=F32)

    for p in range(N_ATTN_HEADS // 2):
        z = proj(2 * p * HEAD_DIM, (2 * p + 2) * HEAD_DIM)
        for j in range(2):
            hd = 2 * p + j
            qh = _head_norm_rope(z[:, j * HEAD_DIM:(j + 1) * HEAD_DIM], qn_ref[...], cos, sin)
            qt_ref[hd * HEAD_DIM:(hd + 1) * HEAD_DIM, :] = (qh * q_scale).T.astype(BF16)
    z = proj(Q_END, K_END)
    for j in range(N_KV_HEADS):
        kh = _head_norm_rope(z[:, j * HEAD_DIM:(j + 1) * HEAD_DIM], kn_ref[...], cos, sin)
        k_ref[:, j * HEAD_DIM:(j + 1) * HEAD_DIM] = kh.astype(BF16)
    vt = proj(K_END, V_END).T
    pad_rows = V_ROWS - HEAD_DIM
    ones_row = (lax.broadcasted_iota(jnp.int32, (pad_rows, vt.shape[1]), 0) == 0).astype(BF16)
    for j in range(N_KV_HEADS):
        vt_ref[j * V_ROWS:j * V_ROWS + HEAD_DIM, :] = vt[j * HEAD_DIM:(j + 1) * HEAD_DIM].astype(BF16)
        vt_ref[j * V_ROWS + HEAD_DIM:(j + 1) * V_ROWS, :] = ones_row
    y_ref[...] = proj(V_END, GLU_MID) * _sigmoid(proj(GLU_MID, CONV_END))
    f = proj(CONV_END, IN_WIDTH)
    for g in range(N_FOURIER_GROUPS):
        lanes = slice(g * HEAD_DIM, (g + 1) * HEAD_DIM)
        r = jnp.dot(f[:, lanes].astype(BF16), dft_ref[...], preferred_element_type=F32)
        zr_ref[:, lanes] = r[:, :HEAD_DIM].astype(BF16)
        zi_ref[:, lanes] = r[:, HEAD_DIM:].astype(BF16)


def _in_projection(x, norm_g, mods, w_in, q_norm, k_norm, cos, sin, dft_c, *, layer, row):
    n, d = x.shape
    tm = _pick_tile(n, 512)
    lyr = lambda i: (layer, 0, 0)
    rows = lambda i: (i, 0)
    kern = functools.partial(_inproj_kernel, row=row, d=d, q_scale=HEAD_DIM ** -0.5 * LOG2E)
    z_spec = pl.BlockSpec((tm, FOURIER_WIDTH), rows)
    z_shape = jax.ShapeDtypeStruct((n, FOURIER_WIDTH), BF16)
    return pl.pallas_call(
        kern,
        grid=(n // tm,),
        in_specs=[
            pl.BlockSpec((tm, d), rows),
            pl.BlockSpec((None, 1, d), lyr),
            pl.BlockSpec((None, SUBLANES_F32, N_MOD * d), lyr),
            _resident((None, d, IN_WIDTH), lyr),
            pl.BlockSpec((None, 1, HEAD_DIM), lyr),
            pl.BlockSpec((None, 1, HEAD_DIM), lyr),
            pl.BlockSpec((tm, HEAD_DIM), rows),
            pl.BlockSpec((tm, HEAD_DIM), rows),
            pl.BlockSpec((HEAD_DIM, 2 * HEAD_DIM), lambda i: (0, 0)),
        ],
        out_specs=[
            pl.BlockSpec((ATTN_WIDTH, tm), lambda i: (0, i)),
            pl.BlockSpec((tm, KV_WIDTH), rows),
            pl.BlockSpec((N_KV_HEADS * V_ROWS, tm), lambda i: (0, i)),
            pl.BlockSpec((tm, CONV_WIDTH), rows),
            z_spec,
            z_spec,
        ],
        out_shape=[
            jax.ShapeDtypeStruct((ATTN_WIDTH, n), BF16),
            jax.ShapeDtypeStruct((n, KV_WIDTH), BF16),
            jax.ShapeDtypeStruct((N_KV_HEADS * V_ROWS, n), BF16),
            jax.ShapeDtypeStruct((n, CONV_WIDTH), F32),
            z_shape,
            z_shape,
        ],
        compiler_params=_cparams(("parallel",)),
        name="in_projection",
    )(x, norm_g, mods, w_in, q_norm, k_norm, cos, sin, dft_c)


def _attn_kernel(*refs, tk, n_lat_chunks, unroll):
    if n_lat_chunks:
        qt_ref, kc_ref, vtc_ref, kl_ref, vtl_ref, o_ref, acc_sc, m_sc, alpha_sc, p_even, p_odd = refs
    else:
        qt_ref, kc_ref, vtc_ref, o_ref = refs
    heads = range(GROUP)

    def q_t(h):
        return qt_ref[h * HEAD_DIM:(h + 1) * HEAD_DIM, :]

    ms, accs = [], []
    for h in heads:
        s = jnp.dot(kc_ref[...], q_t(h), preferred_element_type=F32)
        m = jnp.max(s, axis=0, keepdims=True)
        p = jnp.exp2(s - m)
        ms.append(m)
        accs.append(jnp.dot(vtc_ref[...], p.astype(BF16), preferred_element_type=F32))

    if n_lat_chunks:
        for h in heads:
            acc_sc[h] = accs[h]
            m_sc[h] = ms[h]
            alpha_sc[h] = jnp.ones_like(ms[h])
        p_bufs = (p_even, p_odd)
        p_odd[...] = jnp.zeros(p_odd.shape, BF16)

        def value_update(c, h, p_src):
            off = pl.multiple_of(c * tk, tk)
            acc_sc[h] = alpha_sc[h] * acc_sc[h] + jnp.dot(vtl_ref[:, pl.ds(off, tk)], p_src[h],
                                                          preferred_element_type=F32)

        def step(c, p_dst, p_src):
            off = pl.multiple_of(c * tk, tk)
            k = kl_ref[pl.ds(off, tk), :]
            ss = [jnp.dot(k, q_t(h), preferred_element_type=F32) for h in heads]
            for h in heads:
                value_update(jnp.maximum(c - 1, 0), h, p_src)
                m_old = m_sc[h]
                m_new = jnp.maximum(m_old, jnp.max(ss[h], axis=0, keepdims=True))
                alpha_sc[h] = jnp.exp2(m_old - m_new)
                m_sc[h] = m_new
                p_dst[h] = jnp.exp2(ss[h] - m_new).astype(BF16)

        def body(i, carry):
            for u in range(unroll):
                step(unroll * i + u, p_bufs[u % 2], p_bufs[(u + 1) % 2])
            return carry

        lax.fori_loop(0, n_lat_chunks // unroll, body, 0)
        for h in heads:
            value_update(n_lat_chunks - 1, h, p_odd)
        accs = [acc_sc[h] for h in heads]
    for h in heads:
        out = accs[h][:HEAD_DIM] / accs[h][HEAD_DIM:HEAD_DIM + 1]
        o_ref[:, h * HEAD_DIM:(h + 1) * HEAD_DIM] = out.T.astype(BF16)


def _attention(qt, k_c, vt_c, k_l=None, vt_l=None):
    n = qt.shape[1]
    nc = k_c.shape[0]
    tq = _pick_tile(n, 512)
    qw = GROUP * HEAD_DIM
    in_specs = [
        pl.BlockSpec((qw, tq), lambda k, i: (k, i)),
        pl.BlockSpec((nc, HEAD_DIM), lambda k, i: (0, k)),
        pl.BlockSpec((V_ROWS, nc), lambda k, i: (k, 0)),
    ]
    args = [qt, k_c, vt_c]
    scratch = []
    tk, n_chunks, unroll = 0, 0, 0
    if k_l is not None:
        nl = k_l.shape[0]
        tk = _pick_tile(nl, 512)
        n_chunks = nl // tk
        unroll = 4 if n_chunks % 4 == 0 else 2
        assert n_chunks % unroll == 0, "latent key chunks are processed in pairs"
        in_specs += [
            pl.BlockSpec((nl, HEAD_DIM), lambda k, i: (0, k)),
            pl.BlockSpec((V_ROWS, nl), lambda k, i: (k, 0)),
        ]
        args += [k_l, vt_l]
        scratch = [pltpu.VMEM((GROUP, V_ROWS, tq), F32),
                   pltpu.VMEM((GROUP, 1, tq), F32), pltpu.VMEM((GROUP, 1, tq), F32),
                   pltpu.VMEM((GROUP, tk, tq), BF16), pltpu.VMEM((GROUP, tk, tq), BF16)]
    return pl.pallas_call(
        functools.partial(_attn_kernel, tk=tk, n_lat_chunks=n_chunks, unroll=unroll),
        grid=(N_KV_HEADS, n // tq),
        in_specs=in_specs,
        out_specs=pl.BlockSpec((tq, qw), lambda k, i: (i, k)),
        out_shape=jax.ShapeDtypeStruct((n, ATTN_WIDTH), BF16),
        scratch_shapes=scratch,
        compiler_params=_cparams(("parallel", "parallel")),
        name="attention",
    )(*args)


def _conv_kernel(yp_ref, yc_ref, yn_ref, dw_ref, g_ref, b_ref, pw_ref, o_ref, ext_sc, act_sc,
                 *, tm, n_tiles, chunk):
    i = pl.program_id(0)
    ext_sc[0, 0:HALO, :] = jnp.where(i > 0, yp_ref[...], 0.0)
    ext_sc[0, HALO:HALO + tm, :] = yc_ref[...]
    ext_sc[0, HALO + tm:2 * HALO + tm, :] = jnp.where(i < n_tiles - 1, yn_ref[...], 0.0)
    n_shift_rows = tm + 2 * HALO - SUBLANES_F32
    for s in range(1, SUBLANES_F32):
        ext_sc[s, 0:n_shift_rows, :] = ext_sc[0, s:s + n_shift_rows, :]

    def body(c, carry):
        r0 = pl.multiple_of(c * chunk, chunk)
        acc = jnp.zeros((chunk, CONV_WIDTH), F32)
        for k in range(CONV_KERNEL):
            off = k + HALO - CONV_PAD
            base = pl.multiple_of(r0 + (off // SUBLANES_F32) * SUBLANES_F32, SUBLANES_F32)
            acc = acc + dw_ref[k:k + 1, :] * ext_sc[off % SUBLANES_F32, pl.ds(base, chunk), :]
        mu = jnp.mean(acc, axis=-1, keepdims=True)
        xc = acc - mu
        var = jnp.mean(xc * xc, axis=-1, keepdims=True)
        yn = xc * lax.rsqrt(var + NORM_EPS) * g_ref[...] + b_ref[...]
        act_sc[pl.ds(r0, chunk), :] = (yn * _sigmoid(yn)).astype(BF16)
        return carry

    lax.fori_loop(0, tm // chunk, body, 0)
    o_ref[...] = jnp.dot(act_sc[...], pw_ref[...], preferred_element_type=F32).astype(BF16)


def _conv_module(y, conv_dw, ln_g, ln_b, conv_pw, *, layer):
    n = y.shape[0]
    tm = _pick_tile(n, 256)
    n_tiles = n // tm
    hb = tm // HALO
    lyr = lambda i: (layer, 0, 0)
    chunk = _pick_tile(tm, 64)
    return pl.pallas_call(
        functools.partial(_conv_kernel, tm=tm, n_tiles=n_tiles, chunk=chunk),
        grid=(n_tiles,),
        in_specs=[
            pl.BlockSpec((HALO, CONV_WIDTH), lambda i: (jnp.maximum(i * hb - 1, 0), 0)),
            pl.BlockSpec((tm, CONV_WIDTH), lambda i: (i, 0)),
            pl.BlockSpec((HALO, CONV_WIDTH), lambda i: (jnp.minimum((i + 1) * hb, n // HALO - 1), 0)),
            pl.BlockSpec((None, CONV_KERNEL, CONV_WIDTH), lyr),
            pl.BlockSpec((None, 1, CONV_WIDTH), lyr),
            pl.BlockSpec((None, 1, CONV_WIDTH), lyr),
            pl.BlockSpec((None, CONV_WIDTH, CONV_WIDTH), lyr),
        ],
        out_specs=pl.BlockSpec((tm, CONV_WIDTH), lambda i: (i, 0)),
        out_shape=jax.ShapeDtypeStruct((n, CONV_WIDTH), BF16),
        scratch_shapes=[
            pltpu.VMEM((SUBLANES_F32, tm + 2 * HALO, CONV_WIDTH), F32),
            pltpu.VMEM((tm, CONV_WIDTH), BF16),
        ],
        compiler_params=_cparams(("parallel",)),
        name="conv_module",
    )(y, y, y, conv_dw, ln_g, ln_b, conv_pw)


def _dft_tables(n_pos):
    f32 = lambda a: jnp.asarray(a.astype(np.float32))
    k = np.arange(HEAD_DIM)
    ang = 2.0 * np.pi * np.outer(k, k) / HEAD_DIM
    chan = np.concatenate([np.cos(ang), -np.sin(ang)], axis=1) / math.sqrt(HEAD_DIM)
    tabs = dict(chan=f32(chan).astype(BF16))
    if n_pos < DFT_N1 * DFT_MIN_N2:
        a = 2.0 * np.pi * np.outer(np.arange(n_pos), np.arange(n_pos)) / n_pos
        dense = np.concatenate([np.cos(a), np.sin(a)], axis=1) / math.sqrt(n_pos)
        tabs.update(dense=f32(dense).astype(BF16))
        return tabs
    n1 = DFT_N1
    n2 = n_pos // n1
    a1 = 2.0 * np.pi * np.outer(np.arange(n1), np.arange(n1)) / n1
    c1, s1 = np.cos(a1) / math.sqrt(n1), np.sin(a1) / math.sqrt(n1)
    m1 = np.block([[c1, s1], [-s1, c1]])
    at = 2.0 * np.pi * np.outer(np.arange(n2), np.arange(n1)) / n_pos
    a2 = 2.0 * np.pi * np.outer(np.arange(n2), np.arange(n2)) / n2
    g2 = np.concatenate([np.cos(a2), np.sin(a2)], axis=1) / math.sqrt(n2)
    twc = jnp.broadcast_to(f32(np.cos(at))[:, :, None], (n2, n1, HEAD_DIM))
    tws = jnp.broadcast_to(f32(np.sin(at))[:, :, None], (n2, n1, HEAD_DIM))
    tabs.update(n1=n1, n2=n2, m1=f32(m1).astype(BF16), twc=twc, tws=tws, g2=f32(g2).astype(BF16))
    return tabs


def _dft_stage1_kernel(zr_ref, zi_ref, m1_ref, twc_ref, tws_ref, br_ref, bi_ref, *, n1, nb):
    for b in range(nb):
        cols = slice(b * FOURIER_WIDTH, (b + 1) * FOURIER_WIDTH)
        z = jnp.concatenate([zr_ref[:, cols], zi_ref[:, cols]], axis=0)
        a = jnp.dot(m1_ref[...], z, preferred_element_type=F32)
        tc = twc_ref[b]
        ts = tws_ref[b]
        for g in range(N_FOURIER_GROUPS):
            lanes = slice(g * HEAD_DIM, (g + 1) * HEAD_DIM)
            ar = a[:n1, lanes]
            ai = a[n1:, lanes]
            br_ref[b, :, lanes] = (ar * tc + ai * ts).astype(BF16)
            bi_ref[b, :, lanes] = (ai * tc - ar * ts).astype(BF16)


def _dft_stage2_kernel(br_ref, bi_ref, g2_ref, y_ref):
    b = jnp.concatenate([br_ref[...], bi_ref[...]], axis=0)
    y_ref[...] = jnp.dot(g2_ref[...], b, preferred_element_type=F32).astype(BF16)


def _dft_dense_kernel(zr_ref, zi_ref, g_ref, y_ref):
    z = jnp.concatenate([zr_ref[...], zi_ref[...]], axis=0)
    y_ref[...] = jnp.dot(g_ref[...], z, preferred_element_type=F32).astype(BF16)


def _position_dft(zr, zi, tabs):
    n, w = zr.shape
    if "dense" in tabs:
        return pl.pallas_call(
            _dft_dense_kernel,
            out_shape=jax.ShapeDtypeStruct((n, w), BF16),
            compiler_params=pltpu.CompilerParams(vmem_limit_bytes=V7X_VMEM_LIMIT_BYTES),
            name="position_dft_dense",
        )(zr, zi, tabs["dense"])
    n1, n2 = tabs["n1"], tabs["n2"]
    nb = _pick_tile(n2, 8)
    zr2 = zr.reshape(n1, n2 * w)
    zi2 = zi.reshape(n1, n2 * w)
    br, bi = pl.pallas_call(
        functools.partial(_dft_stage1_kernel, n1=n1, nb=nb),
        grid=(n2 // nb,),
        in_specs=[
            pl.BlockSpec((n1, nb * w), lambda j: (0, j)),
            pl.BlockSpec((n1, nb * w), lambda j: (0, j)),
            pl.BlockSpec((2 * n1, 2 * n1), lambda j: (0, 0)),
            pl.BlockSpec((nb, n1, HEAD_DIM), lambda j: (j, 0, 0)),
            pl.BlockSpec((nb, n1, HEAD_DIM), lambda j: (j, 0, 0)),
        ],
        out_specs=[pl.BlockSpec((nb, n1, w), lambda j: (j, 0, 0))] * 2,
        out_shape=[jax.ShapeDtypeStruct((n2, n1, w), BF16)] * 2,
        compiler_params=_cparams(("parallel",)),
        name="position_dft_stage1",
    )(zr2, zi2, tabs["m1"], tabs["twc"], tabs["tws"])
    tn = _pick_tile(n1 * w, 8192)
    y = pl.pallas_call(
        _dft_stage2_kernel,
        grid=(n1 * w // tn,),
        in_specs=[
            pl.BlockSpec((n2, tn), lambda j: (0, j)),
            pl.BlockSpec((n2, tn), lambda j: (0, j)),
            pl.BlockSpec((n2, 2 * n2), lambda j: (0, 0)),
        ],
        out_specs=pl.BlockSpec((n2, tn), lambda j: (0, j)),
        out_shape=jax.ShapeDtypeStruct((n2, n1 * w), BF16),
        compiler_params=_cparams(("parallel",)),
        name="position_dft_stage2",
    )(br.reshape(n2, n1 * w), bi.reshape(n2, n1 * w), tabs["g2"])
    return y.reshape(n, w)


def _outproj_kernel(x_ref, att_ref, cv_ref, y_ref, fw_ref, wo_ref, mod_ref, o_ref, *, row, d):
    fo = jnp.dot(y_ref[...], fw_ref[...], preferred_element_type=F32).astype(BF16)
    acc = jnp.dot(att_ref[...], wo_ref[0:ATTN_WIDTH, :], preferred_element_type=F32)
    acc = acc + jnp.dot(cv_ref[...], wo_ref[ATTN_WIDTH:ATTN_WIDTH + CONV_WIDTH, :], preferred_element_type=F32)
    acc = acc + jnp.dot(fo, wo_ref[ATTN_WIDTH + CONV_WIDTH:MIX_WIDTH, :], preferred_element_type=F32)
    gate = mod_ref[row:row + 1, 2 * d:3 * d]
    o_ref[...] = x_ref[...] + gate * acc


def _out_projection(x, att, cv, yf, fourier_w, w_out, mods, *, layer, row):
    n, d = x.shape
    tm = _pick_tile(n, 512)
    lyr = lambda i: (layer, 0, 0)
    rows = lambda i: (i, 0)
    return pl.pallas_call(
        functools.partial(_outproj_kernel, row=row, d=d),
        grid=(n // tm,),
        in_specs=[
            pl.BlockSpec((tm, d), rows),
            pl.BlockSpec((tm, ATTN_WIDTH), rows),
            pl.BlockSpec((tm, CONV_WIDTH), rows),
            pl.BlockSpec((tm, FOURIER_WIDTH), rows),
            _resident((None, FOURIER_WIDTH, FOURIER_WIDTH), lyr),
            _resident((None, MIX_WIDTH, d), lyr),
            pl.BlockSpec((None, SUBLANES_F32, N_MOD * d), lyr),
        ],
        out_specs=pl.BlockSpec((tm, d), rows),
        out_shape=jax.ShapeDtypeStruct((n, d), F32),
        compiler_params=_cparams(("parallel",)),
        name="out_projection",
    )(x, att, cv, yf, fourier_w, w_out, mods)


def _ffn_kernel(xp_ref, x_ref, xn_ref, g_ref, mod_ref, wuv_ref, wug_ref, dwv_ref, dwg_ref, wd_ref, fg_ref,
                o_ref, h_sc, act_sc, *, row, d, tm, n_tiles, n_hid, final_norm):
    i = pl.program_id(0)
    j = pl.program_id(1)

    def up_proj():
        h = h_sc[...]
        ug = jnp.dot(h, wug_ref[...], preferred_element_type=F32)
        uv = jnp.dot(h, wuv_ref[...], preferred_element_type=F32)
        return uv, ug

    def conv_gate(us, slot):
        def conv(u, dw_ref):
            prev = pltpu.roll(u, 1, axis=0)[HALO:HALO + tm]
            nxt = pltpu.roll(u, u.shape[0] - 1, axis=0)[HALO:HALO + tm]
            return dw_ref[0:1, :] * prev + dw_ref[1:2, :] * u[HALO:HALO + tm] + dw_ref[2:3, :] * nxt

        gt = conv(us[1], dwg_ref).astype(BF16)
        silu = gt * _sigmoid(gt)
        act_sc[slot] = silu * conv(us[0], dwv_ref).astype(BF16)

    def down_proj(slot):
        return jnp.dot(act_sc[slot], wd_ref[...], preferred_element_type=F32)

    @pl.when(j == 0)
    def _():
        shift = mod_ref[row:row + 1, 3 * d:4 * d]
        scale = mod_ref[row:row + 1, 4 * d:5 * d]

        def norm(x):
            ms = jnp.mean(x * x, axis=-1, keepdims=True)
            return (x * lax.rsqrt(ms + NORM_EPS) * g_ref[...]) * (1.0 + scale) + shift

        h_sc[0:HALO, :] = jnp.where(i > 0, norm(xp_ref[...]), 0.0).astype(BF16)
        h_sc[HALO:HALO + tm, :] = norm(x_ref[...]).astype(BF16)
        h_sc[HALO + tm:2 * HALO + tm, :] = jnp.where(i < n_tiles - 1, norm(xn_ref[...]), 0.0).astype(BF16)
        o_ref[...] = jnp.zeros(o_ref.shape, F32)
        conv_gate(up_proj(), 0)

    for parity in range(2):
        @pl.when((j > 0) & (j < n_hid) & (j % 2 == parity))
        def _(parity=parity):
            us = up_proj()
            o_ref[...] += down_proj(1 - parity)
            conv_gate(us, parity)

    @pl.when(j == n_hid)
    def _():
        gate = mod_ref[row:row + 1, 5 * d:6 * d]
        y = x_ref[...] + gate * (o_ref[...] + down_proj((n_hid - 1) % 2))
        if final_norm:
            y = y * lax.rsqrt(jnp.mean(y * y, axis=-1, keepdims=True) + NORM_EPS) * fg_ref[...]
        o_ref[...] = y


def _conv_ffn(x, norm_g, mods, w_up, ffn_dw, w_down, final_g, *, layer, row, final_norm):
    n, d = x.shape
    hidden = w_down.shape[1]
    tm = _pick_tile(n, 1024)
    th = 512 if hidden % 512 == 0 else 256
    n_tiles, n_hid = n // tm, hidden // th
    hb = tm // HALO
    lyr = lambda i, j: (layer, 0, 0)
    up_tile = lambda j: jnp.minimum(j, n_hid - 1)
    down_tile = lambda j: jnp.maximum(j - 1, 0)
    return pl.pallas_call(
        functools.partial(_ffn_kernel, row=row, d=d, tm=tm, n_tiles=n_tiles, n_hid=n_hid, final_norm=final_norm),
        grid=(n_tiles, n_hid + 1),
        in_specs=[
            pl.BlockSpec((HALO, d), lambda i, j: (jnp.maximum(i * hb - 1, 0), 0)),
            pl.BlockSpec((tm, d), lambda i, j: (i, 0), pipeline_mode=pl.Buffered(1)),
            pl.BlockSpec((HALO, d), lambda i, j: (jnp.minimum((i + 1) * hb, n // HALO - 1), 0)),
            pl.BlockSpec((None, 1, d), lyr),
            pl.BlockSpec((None, SUBLANES_F32, N_MOD * d), lyr),
            pl.BlockSpec((None, d, th), lambda i, j: (layer, 0, up_tile(j))),
            pl.BlockSpec((None, d, th), lambda i, j: (layer, 0, n_hid + up_tile(j))),
            pl.BlockSpec((None, FFN_CONV_KERNEL, th), lambda i, j: (layer, 0, up_tile(j))),
            pl.BlockSpec((None, FFN_CONV_KERNEL, th), lambda i, j: (layer, 0, n_hid + up_tile(j))),
            pl.BlockSpec((None, th, d), lambda i, j: (layer, down_tile(j), 0)),
            pl.BlockSpec((1, d), lambda i, j: (0, 0)),
        ],
        out_specs=pl.BlockSpec((tm, d), lambda i, j: (i, 0)),
        out_shape=jax.ShapeDtypeStruct((n, d), F32),
        scratch_shapes=[pltpu.VMEM((tm + 2 * HALO, d), BF16), pltpu.VMEM((2, tm, th), BF16)],
        compiler_params=_cparams(("parallel", "arbitrary")),
        name="conv_ffn",
    )(x, x, x, norm_g, mods, w_up, w_up, ffn_dw, ffn_dw, w_down, final_g)


def _rope_tables(n_tokens):
    rows = n_tokens // GRID_W
    row = jnp.repeat(jnp.arange(rows), GRID_W).astype(F32)
    col = jnp.tile(jnp.arange(GRID_W), rows).astype(F32)
    n_freq = HEAD_DIM // 4
    inv_freq = ROPE_THETA ** (-jnp.arange(n_freq, dtype=F32) / n_freq)
    ang = jnp.concatenate([row[:, None] * inv_freq, col[:, None] * inv_freq], axis=-1)
    cos, sin = jnp.cos(ang), jnp.sin(ang)
    return jnp.concatenate([cos, cos], axis=-1), jnp.concatenate([-sin, sin], axis=-1)


def kernel(x, c, ctx, c_ctx, w_mod, b_mod, mix_norm, ffn_norm, w_in, q_norm, k_norm, conv_dw, conv_ln_g,
           conv_ln_b, conv_pw, fourier_w, w_out, ffn_up, ffn_dw, ffn_down, final_norm):
    batch, n_lat, d = x.shape
    n_ctx = ctx.shape[1]
    depth = w_mod.shape[0]
    assert batch == 1 and c.shape == (1, d)

    ct = jnp.stack([c[0], c_ctx], axis=1)
    mods = _modulation(ct, w_mod, b_mod)

    w_in_b, w_out_b = w_in.astype(BF16), w_out.astype(BF16)
    ffn_up_b, ffn_down_b = ffn_up.astype(BF16), ffn_down.astype(BF16)
    conv_pw_b, fourier_w_b = conv_pw.astype(BF16), fourier_w.astype(BF16)
    vec = lambda a: a.reshape(depth, 1, a.shape[-1])
    mix_g, ffn_g, qn, kn = vec(mix_norm), vec(ffn_norm), vec(q_norm), vec(k_norm)
    ln_g, ln_b = vec(conv_ln_g), vec(conv_ln_b)
    final_g = final_norm.reshape(1, d)

    rope_l = _rope_tables(n_lat)
    rope_c = (jnp.ones((n_ctx, HEAD_DIM), F32), jnp.zeros((n_ctx, HEAD_DIM), F32))
    tabs_l, tabs_c = _dft_tables(n_lat), _dft_tables(n_ctx)

    xl, xc = x[0], ctx[0]
    for l in range(depth):
        last = l == depth - 1
        qtc, kc, vtc, yc, zrc, zic = _in_projection(xc, mix_g, mods, w_in_b, qn, kn, *rope_c, tabs_c["chan"],
                                                    layer=l, row=1)
        qtl, kl, vtl, yl, zrl, zil = _in_projection(xl, mix_g, mods, w_in_b, qn, kn, *rope_l, tabs_l["chan"],
                                                    layer=l, row=0)
        att = _attention(qtl, kc, vtc, kl, vtl)
        cv = _conv_module(yl, conv_dw, ln_g, ln_b, conv_pw_b, layer=l)
        yf = _position_dft(zrl, zil, tabs_l)
        xl = _out_projection(xl, att, cv, yf, fourier_w_b, w_out_b, mods, layer=l, row=0)
        xl = _conv_ffn(xl, ffn_g, mods, ffn_up_b, ffn_dw, ffn_down_b, final_g, layer=l, row=0, final_norm=last)
        if not last:
            att = _attention(qtc, kc, vtc)
            cv = _conv_module(yc, conv_dw, ln_g, ln_b, conv_pw_b, layer=l)
            yf = _position_dft(zrc, zic, tabs_c)
            xc = _out_projection(xc, att, cv, yf, fourier_w_b, w_out_b, mods, layer=l, row=1)
            xc = _conv_ffn(xc, ffn_g, mods, ffn_up_b, ffn_dw, ffn_down_b, final_g, layer=l, row=1,
                           final_norm=False)
    return xl[None]
```

```python
import functools
import math

import numpy as np
import jax
import jax.numpy as jnp
from jax import lax
from jax.experimental import pallas as pl
from jax.experimental.pallas import tpu as pltpu

F32 = jnp.float32
BF16 = jnp.bfloat16

HEAD_DIM = 128
N_ATTN_HEADS = 8
N_KV_HEADS = 2
GROUP = N_ATTN_HEADS // N_KV_HEADS
ATTN_WIDTH = N_ATTN_HEADS * HEAD_DIM
KV_WIDTH = N_KV_HEADS * HEAD_DIM
CONV_WIDTH = 4 * HEAD_DIM
FOURIER_WIDTH = 4 * HEAD_DIM
N_FOURIER_GROUPS = 4
Q_END = ATTN_WIDTH
K_END = Q_END + KV_WIDTH
V_END = K_END + KV_WIDTH
GLU_MID = V_END + CONV_WIDTH
CONV_END = V_END + 2 * CONV_WIDTH
IN_WIDTH = CONV_END + FOURIER_WIDTH
MIX_WIDTH = ATTN_WIDTH + CONV_WIDTH + FOURIER_WIDTH
CONV_KERNEL = 31
CONV_PAD = (CONV_KERNEL - 1) // 2
FFN_CONV_KERNEL = 3
GRID_W = 64
ROPE_THETA = 10000.0
NORM_EPS = 1e-6
N_MOD = 6
LOG2E = 1.4426950408889634

V7X_VMEM_LIMIT_BYTES = 56 * 1024 * 1024
SUBLANES_F32 = 8
HALO = 16
DFT_N1 = 128
DFT_MIN_N2 = 16
V_ROWS = HEAD_DIM + 16


def _cparams(sem):
    return pltpu.CompilerParams(dimension_semantics=sem, vmem_limit_bytes=V7X_VMEM_LIMIT_BYTES)


def _resident(shape, index_map):
    return pl.BlockSpec(shape, index_map, pipeline_mode=pl.Buffered(1))


def _sigmoid(x):
    return 1.0 / (1.0 + jnp.exp(-x))


def _pick_tile(n, pref):
    t = min(n, pref)
    while n % t:
        t //= 2
    return t


def _mod_kernel(ct_ref, w_ref, b_ref, o_ref):
    ct = ct_ref[...]
    s = ct * _sigmoid(ct)
    w = w_ref[...]
    rows = [jnp.sum(s[:, r:r + 1] * w, axis=0, keepdims=True) for r in range(2)]
    rows.append(jnp.zeros((SUBLANES_F32 - 2, w.shape[1]), F32))
    o_ref[...] = jnp.concatenate(rows, axis=0) + b_ref[...]


def _modulation(ct, w_mod, b_mod):
    depth, d, n = w_mod.shape
    tn = _pick_tile(n, 1024)
    return pl.pallas_call(
        _mod_kernel,
        grid=(depth, n // tn),
        in_specs=[
            pl.BlockSpec((d, 2), lambda l, j: (0, 0)),
            pl.BlockSpec((None, d, tn), lambda l, j: (l, 0, j)),
            pl.BlockSpec((None, 1, tn), lambda l, j: (l, 0, j)),
        ],
        out_specs=pl.BlockSpec((None, SUBLANES_F32, tn), lambda l, j: (l, 0, j)),
        out_shape=jax.ShapeDtypeStruct((depth, SUBLANES_F32, n), F32),
        compiler_params=_cparams(("arbitrary", "arbitrary")),
        name="adaln_modulation",
    )(ct, w_mod, b_mod.reshape(depth, 1, n))


def _head_norm_rope(z, g, cos, sin):
    y = z * lax.rsqrt(jnp.mean(z * z, axis=-1, keepdims=True) + NORM_EPS) * g
    return y * cos + pltpu.roll(y, HEAD_DIM // 2, axis=1) * sin


def _inproj_kernel(x_ref, g_ref, mod_ref, w_ref, qn_ref, kn_ref, cos_ref, sin_ref, dft_ref,
                   qt_ref, k_ref, vt_ref, y_ref, zr_ref, zi_ref, *, row, d, q_scale):
    x = x_ref[...]
    shift = mod_ref[row:row + 1, 0:d]
    scale = mod_ref[row:row + 1, d:2 * d]
    ms = jnp.mean(x * x, axis=-1, keepdims=True)
    h = (x * lax.rsqrt(ms + NORM_EPS) * g_ref[...]) * (1.0 + scale) + shift
    hb = h.astype(BF16)
    cos = cos_ref[...]
    sin = sin_ref[...]

    def proj(lo, hi):
        return jnp.dot(hb, w_ref[:, lo:hi], preferred_element_type=F32)

    for p in range(N_ATTN_HEADS // 2):
        z = proj(2 * p * HEAD_DIM, (2 * p + 2) * HEAD_DIM)
        for j in range(2):
            hd = 2 * p + j
            qh = _head_norm_rope(z[:, j * HEAD_DIM:(j + 1) * HEAD_DIM], qn_ref[...], cos, sin)
            qt_ref[hd * HEAD_DIM:(hd + 1) * HEAD_DIM, :] = (qh * q_scale).T.astype(BF16)
    z = proj(Q_END, K_END)
    for j in range(N_KV_HEADS):
        kh = _head_norm_rope(z[:, j * HEAD_DIM:(j + 1) * HEAD_DIM], kn_ref[...], cos, sin)
        k_ref[:, j * HEAD_DIM:(j + 1) * HEAD_DIM] = kh.astype(BF16)
    vt = proj(K_END, V_END).T
    pad_rows = V_ROWS - HEAD_DIM
    ones_row = (lax.broadcasted_iota(jnp.int32, (pad_rows, vt.shape[1]), 0) == 0).astype(BF16)
    for j in range(N_KV_HEADS):
        vt_ref[j * V_ROWS:j * V_ROWS + HEAD_DIM, :] = vt[j * HEAD_DIM:(j + 1) * HEAD_DIM].astype(BF16)
        vt_ref[j * V_ROWS + HEAD_DIM:(j + 1) * V_ROWS, :] = ones_row
    y_ref[...] = proj(V_END, GLU_MID) * _sigmoid(proj(GLU_MID, CONV_END))
    f = proj(CONV_END, IN_WIDTH)
    for g in range(N_FOURIER_GROUPS):
        lanes = slice(g * HEAD_DIM, (g + 1) * HEAD_DIM)
        r = jnp.dot(f[:, lanes].astype(BF16), dft_ref[...], preferred_element_type=F32)
        zr_ref[:, lanes] = r[:, :HEAD_DIM].astype(BF16)
        zi_ref[:, lanes] = r[:, HEAD_DIM:].astype(BF16)


def _in_projection(x, norm_g, mods, w_in, q_norm, k_norm, cos, sin, dft_c, *, layer, row):
    n, d = x.shape
    tm = _pick_tile(n, 512)
    lyr = lambda i: (layer, 0, 0)
    rows = lambda i: (i, 0)
    kern = functools.partial(_inproj_kernel, row=row, d=d, q_scale=HEAD_DIM ** -0.5 * LOG2E)
    z_spec = pl.BlockSpec((tm, FOURIER_WIDTH), rows)
    z_shape = jax.ShapeDtypeStruct((n, FOURIER_WIDTH), BF16)
    return pl.pallas_call(
        kern,
        grid=(n // tm,),
        in_specs=[
            pl.BlockSpec((tm, d), rows),
            pl.BlockSpec((None, 1, d), lyr),
            pl.BlockSpec((None, SUBLANES_F32, N_MOD * d), lyr),
            _resident((None, d, IN_WIDTH), lyr),
            pl.BlockSpec((None, 1, HEAD_DIM), lyr),
            pl.BlockSpec((None, 1, HEAD_DIM), lyr),
            pl.BlockSpec((tm, HEAD_DIM), rows),
            pl.BlockSpec((tm, HEAD_DIM), rows),
            pl.BlockSpec((HEAD_DIM, 2 * HEAD_DIM), lambda i: (0, 0)),
        ],
        out_specs=[
            pl.BlockSpec((ATTN_WIDTH, tm), lambda i: (0, i)),
            pl.BlockSpec((tm, KV_WIDTH), rows),
            pl.BlockSpec((N_KV_HEADS * V_ROWS, tm), lambda i: (0, i)),
            pl.BlockSpec((tm, CONV_WIDTH), rows),
            z_spec,
            z_spec,
        ],
        out_shape=[
            jax.ShapeDtypeStruct((ATTN_WIDTH, n), BF16),
            jax.ShapeDtypeStruct((n, KV_WIDTH), BF16),
            jax.ShapeDtypeStruct((N_KV_HEADS * V_ROWS, n), BF16),
            jax.ShapeDtypeStruct((n, CONV_WIDTH), F32),
            z_shape,
            z_shape,
        ],
        compiler_params=_cparams(("parallel",)),
        name="in_projection",
    )(x, norm_g, mods, w_in, q_norm, k_norm, cos, sin, dft_c)


def _attn_kernel(*refs, tk, n_lat_chunks, unroll):
    if n_lat_chunks:
        qt_ref, kc_ref, vtc_ref, kl_ref, vtl_ref, o_ref, acc_sc, m_sc, alpha_sc, p_even, p_odd = refs
    else:
        qt_ref, kc_ref, vtc_ref, o_ref = refs
    heads = range(GROUP)

    def q_t(h):
        return qt_ref[h * HEAD_DIM:(h + 1) * HEAD_DIM, :]

    ss = [jnp.dot(kc_ref[...], q_t(h), preferred_element_type=F32) for h in heads]
    ms = [jnp.max(s, axis=0, keepdims=True) for s in ss]
    ps = [jnp.exp2(s - m).astype(BF16) for s, m in zip(ss, ms)]
    accs = [jnp.dot(vtc_ref[...], p, preferred_element_type=F32) for p in ps]

    if n_lat_chunks:
        for h in heads:
            acc_sc[h] = accs[h]
            m_sc[h] = ms[h]
            alpha_sc[h] = jnp.ones_like(ms[h])
        p_bufs = (p_even, p_odd)
        p_odd[...] = jnp.zeros(p_odd.shape, BF16)

        def value_update(c, h, p_src):
            off = pl.multiple_of(c * tk, tk)
            acc_sc[h] = alpha_sc[h] * acc_sc[h] + jnp.dot(vtl_ref[:, pl.ds(off, tk)], p_src[h],
                                                          preferred_element_type=F32)

        def step(c, p_dst, p_src):
            off = pl.multiple_of(c * tk, tk)
            k = kl_ref[pl.ds(off, tk), :]
            ss = [jnp.dot(k, q_t(h), preferred_element_type=F32) for h in heads]
            off_prev = pl.multiple_of(jnp.maximum(c - 1, 0) * tk, tk)
            vt = vtl_ref[:, pl.ds(off_prev, tk)]
            pvs = [jnp.dot(vt, p_src[h], preferred_element_type=F32) for h in heads]
            for h in heads:
                acc_sc[h] = alpha_sc[h] * acc_sc[h] + pvs[h]
                m_old = m_sc[h]
                m_new = jnp.maximum(m_old, jnp.max(ss[h], axis=0, keepdims=True))
                alpha_sc[h] = jnp.exp2(m_old - m_new)
                m_sc[h] = m_new
                p_dst[h] = jnp.exp2(ss[h] - m_new).astype(BF16)

        def body(i, carry):
            for u in range(unroll):
                step(unroll * i + u, p_bufs[u % 2], p_bufs[(u + 1) % 2])
            return carry

        lax.fori_loop(0, n_lat_chunks // unroll, body, 0)
        for h in heads:
            value_update(n_lat_chunks - 1, h, p_odd)
        accs = [acc_sc[h] for h in heads]
    for h in heads:
        out = accs[h][:HEAD_DIM] / accs[h][HEAD_DIM:HEAD_DIM + 1]
        o_ref[:, h * HEAD_DIM:(h + 1) * HEAD_DIM] = out.T.astype(BF16)


def _attention(qt, k_c, vt_c, k_l=None, vt_l=None):
    n = qt.shape[1]
    nc = k_c.shape[0]
    tq = _pick_tile(n, 512)
    qw = GROUP * HEAD_DIM
    in_specs = [
        pl.BlockSpec((qw, tq), lambda k, i: (k, i)),
        pl.BlockSpec((nc, HEAD_DIM), lambda k, i: (0, k)),
        pl.BlockSpec((V_ROWS, nc), lambda k, i: (k, 0)),
    ]
    args = [qt, k_c, vt_c]
    scratch = []
    tk, n_chunks, unroll = 0, 0, 0
    if k_l is not None:
        nl = k_l.shape[0]
        tk = _pick_tile(nl, 512)
        n_chunks = nl // tk
        unroll = 4 if n_chunks % 4 == 0 else 2
        assert n_chunks % unroll == 0, "latent key chunks are processed in pairs"
        in_specs += [
            pl.BlockSpec((nl, HEAD_DIM), lambda k, i: (0, k)),
            pl.BlockSpec((V_ROWS, nl), lambda k, i: (k, 0)),
        ]
        args += [k_l, vt_l]
        scratch = [pltpu.VMEM((GROUP, V_ROWS, tq), F32),
                   pltpu.VMEM((GROUP, 1, tq), F32), pltpu.VMEM((GROUP, 1, tq), F32),
                   pltpu.VMEM((GROUP, tk, tq), BF16), pltpu.VMEM((GROUP, tk, tq), BF16)]
    return pl.pallas_call(
        functools.partial(_attn_kernel, tk=tk, n_lat_chunks=n_chunks, unroll=unroll),
        grid=(N_KV_HEADS, n // tq),
        in_specs=in_specs,
        out_specs=pl.BlockSpec((tq, qw), lambda k, i: (i, k)),
        out_shape=jax.ShapeDtypeStruct((n, ATTN_WIDTH), BF16),
        scratch_shapes=scratch,
        compiler_params=_cparams(("parallel", "parallel")),
        name="attention",
    )(*args)


def _conv_kernel(yp_ref, yc_ref, yn_ref, dw_ref, g_ref, b_ref, pw_ref, o_ref, ext_sc, act_sc,
                 *, tm, n_tiles, chunk):
    i = pl.program_id(0)
    ext_sc[0, 0:HALO, :] = jnp.where(i > 0, yp_ref[...], 0.0)
    ext_sc[0, HALO:HALO + tm, :] = yc_ref[...]
    ext_sc[0, HALO + tm:2 * HALO + tm, :] = jnp.where(i < n_tiles - 1, yn_ref[...], 0.0)
    n_shift_rows = tm + 2 * HALO - SUBLANES_F32
    for s in range(1, SUBLANES_F32):
        ext_sc[s, 0:n_shift_rows, :] = ext_sc[0, s:s + n_shift_rows, :]

    def body(c, carry):
        r0 = pl.multiple_of(c * chunk, chunk)
        acc = jnp.zeros((chunk, CONV_WIDTH), F32)
        for k in range(CONV_KERNEL):
            off = k + HALO - CONV_PAD
            base = pl.multiple_of(r0 + (off // SUBLANES_F32) * SUBLANES_F32, SUBLANES_F32)
            acc = acc + dw_ref[k:k + 1, :] * ext_sc[off % SUBLANES_F32, pl.ds(base, chunk), :]
        mu = jnp.mean(acc, axis=-1, keepdims=True)
        xc = acc - mu
        var = jnp.mean(xc * xc, axis=-1, keepdims=True)
        yn = xc * lax.rsqrt(var + NORM_EPS) * g_ref[...] + b_ref[...]
        act_sc[pl.ds(r0, chunk), :] = (yn * _sigmoid(yn)).astype(BF16)
        return carry

    lax.fori_loop(0, tm // chunk, body, 0)
    o_ref[...] = jnp.dot(act_sc[...], pw_ref[...], preferred_element_type=F32).astype(BF16)


def _conv_module(y, conv_dw, ln_g, ln_b, conv_pw, *, layer):
    n = y.shape[0]
    tm = _pick_tile(n, 256)
    n_tiles = n // tm
    hb = tm // HALO
    lyr = lambda i: (layer, 0, 0)
    chunk = _pick_tile(tm, 64)
    return pl.pallas_call(
        functools.partial(_conv_kernel, tm=tm, n_tiles=n_tiles, chunk=chunk),
        grid=(n_tiles,),
        in_specs=[
            pl.BlockSpec((HALO, CONV_WIDTH), lambda i: (jnp.maximum(i * hb - 1, 0), 0)),
            pl.BlockSpec((tm, CONV_WIDTH), lambda i: (i, 0)),
            pl.BlockSpec((HALO, CONV_WIDTH), lambda i: (jnp.minimum((i + 1) * hb, n // HALO - 1), 0)),
            pl.BlockSpec((None, CONV_KERNEL, CONV_WIDTH), lyr),
            pl.BlockSpec((None, 1, CONV_WIDTH), lyr),
            pl.BlockSpec((None, 1, CONV_WIDTH), lyr),
            pl.BlockSpec((None, CONV_WIDTH, CONV_WIDTH), lyr),
        ],
        out_specs=pl.BlockSpec((tm, CONV_WIDTH), lambda i: (i, 0)),
        out_shape=jax.ShapeDtypeStruct((n, CONV_WIDTH), BF16),
        scratch_shapes=[
            pltpu.VMEM((SUBLANES_F32, tm + 2 * HALO, CONV_WIDTH), F32),
            pltpu.VMEM((tm, CONV_WIDTH), BF16),
        ],
        compiler_params=_cparams(("parallel",)),
        name="conv_module",
    )(y, y, y, conv_dw, ln_g, ln_b, conv_pw)


def _dft_tables(n_pos):
    f32 = lambda a: jnp.asarray(a.astype(np.float32))
    k = np.arange(HEAD_DIM)
    ang = 2.0 * np.pi * np.outer(k, k) / HEAD_DIM
    chan = np.concatenate([np.cos(ang), -np.sin(ang)], axis=1) / math.sqrt(HEAD_DIM)
    tabs = dict(chan=f32(chan).astype(BF16))
    if n_pos < DFT_N1 * DFT_MIN_N2:
        a = 2.0 * np.pi * np.outer(np.arange(n_pos), np.arange(n_pos)) / n_pos
        dense = np.concatenate([np.cos(a), np.sin(a)], axis=1) / math.sqrt(n_pos)
        tabs.update(dense=f32(dense).astype(BF16))
        return tabs
    n1 = DFT_N1
    n2 = n_pos // n1
    a1 = 2.0 * np.pi * np.outer(np.arange(n1), np.arange(n1)) / n1
    c1, s1 = np.cos(a1) / math.sqrt(n1), np.sin(a1) / math.sqrt(n1)
    m1 = np.block([[c1, s1], [-s1, c1]])
    at = 2.0 * np.pi * np.outer(np.arange(n2), np.arange(n1)) / n_pos
    a2 = 2.0 * np.pi * np.outer(np.arange(n2), np.arange(n2)) / n2
    g2 = np.concatenate([np.cos(a2), np.sin(a2)], axis=1) / math.sqrt(n2)
    twc = jnp.broadcast_to(f32(np.cos(at))[:, :, None], (n2, n1, HEAD_DIM))
    tws = jnp.broadcast_to(f32(np.sin(at))[:, :, None], (n2, n1, HEAD_DIM))
    tabs.update(n1=n1, n2=n2, m1=f32(m1).astype(BF16), twc=twc, tws=tws, g2=f32(g2).astype(BF16))
    return tabs


def _dft_stage1_kernel(zr_ref, zi_ref, m1_ref, twc_ref, tws_ref, br_ref, bi_ref, *, n1, nb):
    for b in range(nb):
        cols = slice(b * FOURIER_WIDTH, (b + 1) * FOURIER_WIDTH)
        z = jnp.concatenate([zr_ref[:, cols], zi_ref[:, cols]], axis=0)
        a = jnp.dot(m1_ref[...], z, preferred_element_type=F32)
        tc = twc_ref[b]
        ts = tws_ref[b]
        for g in range(N_FOURIER_GROUPS):
            lanes = slice(g * HEAD_DIM, (g + 1) * HEAD_DIM)
            ar = a[:n1, lanes]
            ai = a[n1:, lanes]
            br_ref[b, :, lanes] = (ar * tc + ai * ts).astype(BF16)
            bi_ref[b, :, lanes] = (ai * tc - ar * ts).astype(BF16)


def _dft_stage2_kernel(br_ref, bi_ref, g2_ref, y_ref):
    b = jnp.concatenate([br_ref[...], bi_ref[...]], axis=0)
    y_ref[...] = jnp.dot(g2_ref[...], b, preferred_element_type=F32).astype(BF16)


def _dft_dense_kernel(zr_ref, zi_ref, g_ref, y_ref):
    z = jnp.concatenate([zr_ref[...], zi_ref[...]], axis=0)
    y_ref[...] = jnp.dot(g_ref[...], z, preferred_element_type=F32).astype(BF16)


def _position_dft(zr, zi, tabs):
    n, w = zr.shape
    if "dense" in tabs:
        return pl.pallas_call(
            _dft_dense_kernel,
            out_shape=jax.ShapeDtypeStruct((n, w), BF16),
            compiler_params=pltpu.CompilerParams(vmem_limit_bytes=V7X_VMEM_LIMIT_BYTES),
            name="position_dft_dense",
        )(zr, zi, tabs["dense"])
    n1, n2 = tabs["n1"], tabs["n2"]
    nb = _pick_tile(n2, 8)
    zr2 = zr.reshape(n1, n2 * w)
    zi2 = zi.reshape(n1, n2 * w)
    br, bi = pl.pallas_call(
        functools.partial(_dft_stage1_kernel, n1=n1, nb=nb),
        grid=(n2 // nb,),
        in_specs=[
            pl.BlockSpec((n1, nb * w), lambda j: (0, j)),
            pl.BlockSpec((n1, nb * w), lambda j: (0, j)),
            pl.BlockSpec((2 * n1, 2 * n1), lambda j: (0, 0)),
            pl.BlockSpec((nb, n1, HEAD_DIM), lambda j: (j, 0, 0)),
            pl.BlockSpec((nb, n1, HEAD_DIM), lambda j: (j, 0, 0)),
        ],
        out_specs=[pl.BlockSpec((nb, n1, w), lambda j: (j, 0, 0))] * 2,
        out_shape=[jax.ShapeDtypeStruct((n2, n1, w), BF16)] * 2,
        compiler_params=_cparams(("parallel",)),
        name="position_dft_stage1",
    )(zr2, zi2, tabs["m1"], tabs["twc"], tabs["tws"])
    tn = _pick_tile(n1 * w, 8192)
    y = pl.pallas_call(
        _dft_stage2_kernel,
        grid=(n1 * w // tn,),
        in_specs=[
            pl.BlockSpec((n2, tn), lambda j: (0, j)),
            pl.BlockSpec((n2, tn), lambda j: (0, j)),
            pl.BlockSpec((n2, 2 * n2), lambda j: (0, 0)),
        ],
        out_specs=pl.BlockSpec((n2, tn), lambda j: (0, j)),
        out_shape=jax.ShapeDtypeStruct((n2, n1 * w), BF16),
        compiler_params=_cparams(("parallel",)),
        name="position_dft_stage2",
    )(br.reshape(n2, n1 * w), bi.reshape(n2, n1 * w), tabs["g2"])
    return y.reshape(n, w)


def _outproj_kernel(x_ref, att_ref, cv_ref, y_ref, fw_ref, wo_ref, mod_ref, o_ref, *, row, d):
    fo = jnp.dot(y_ref[...], fw_ref[...], preferred_element_type=F32).astype(BF16)
    acc = jnp.dot(att_ref[...], wo_ref[0:ATTN_WIDTH, :], preferred_element_type=F32)
    acc = acc + jnp.dot(cv_ref[...], wo_ref[ATTN_WIDTH:ATTN_WIDTH + CONV_WIDTH, :], preferred_element_type=F32)
    acc = acc + jnp.dot(fo, wo_ref[ATTN_WIDTH + CONV_WIDTH:MIX_WIDTH, :], preferred_element_type=F32)
    gate = mod_ref[row:row + 1, 2 * d:3 * d]
    o_ref[...] = x_ref[...] + gate * acc


def _out_projection(x, att, cv, yf, fourier_w, w_out, mods, *, layer, row):
    n, d = x.shape
    tm = _pick_tile(n, 512)
    lyr = lambda i: (layer, 0, 0)
    rows = lambda i: (i, 0)
    return pl.pallas_call(
        functools.partial(_outproj_kernel, row=row, d=d),
        grid=(n // tm,),
        in_specs=[
            pl.BlockSpec((tm, d), rows),
            pl.BlockSpec((tm, ATTN_WIDTH), rows),
            pl.BlockSpec((tm, CONV_WIDTH), rows),
            pl.BlockSpec((tm, FOURIER_WIDTH), rows),
            _resident((None, FOURIER_WIDTH, FOURIER_WIDTH), lyr),
            _resident((None, MIX_WIDTH, d), lyr),
            pl.BlockSpec((None, SUBLANES_F32, N_MOD * d), lyr),
        ],
        out_specs=pl.BlockSpec((tm, d), rows),
        out_shape=jax.ShapeDtypeStruct((n, d), F32),
        compiler_params=_cparams(("parallel",)),
        name="out_projection",
    )(x, att, cv, yf, fourier_w, w_out, mods)


def _ffn_kernel(xp_ref, x_ref, xn_ref, g_ref, mod_ref, wuv_ref, wug_ref, dwv_ref, dwg_ref, wd_ref, fg_ref,
                o_ref, h_sc, act_sc, *, row, d, tm, n_tiles, n_hid, final_norm):
    i = pl.program_id(0)
    j = pl.program_id(1)

    def up_proj():
        h = h_sc[...]
        ug = jnp.dot(h, wug_ref[...], preferred_element_type=F32)
        uv = jnp.dot(h, wuv_ref[...], preferred_element_type=F32)
        return uv, ug

    def conv_gate(us, slot):
        def conv(u, dw_ref):
            prev = pltpu.roll(u, 1, axis=0)[HALO:HALO + tm]
            nxt = pltpu.roll(u, u.shape[0] - 1, axis=0)[HALO:HALO + tm]
            return dw_ref[0:1, :] * prev + dw_ref[1:2, :] * u[HALO:HALO + tm] + dw_ref[2:3, :] * nxt

        gt = conv(us[1], dwg_ref).astype(BF16)
        silu = gt * _sigmoid(gt)
        act_sc[slot] = silu * conv(us[0], dwv_ref).astype(BF16)

    def down_proj(slot):
        return jnp.dot(act_sc[slot], wd_ref[...], preferred_element_type=F32)

    @pl.when(j == 0)
    def _():
        shift = mod_ref[row:row + 1, 3 * d:4 * d]
        scale = mod_ref[row:row + 1, 4 * d:5 * d]

        def norm(x):
            ms = jnp.mean(x * x, axis=-1, keepdims=True)
            return (x * lax.rsqrt(ms + NORM_EPS) * g_ref[...]) * (1.0 + scale) + shift

        h_sc[0:HALO, :] = jnp.where(i > 0, norm(xp_ref[...]), 0.0).astype(BF16)
        h_sc[HALO:HALO + tm, :] = norm(x_ref[...]).astype(BF16)
        h_sc[HALO + tm:2 * HALO + tm, :] = jnp.where(i < n_tiles - 1, norm(xn_ref[...]), 0.0).astype(BF16)
        o_ref[...] = jnp.zeros(o_ref.shape, F32)
        conv_gate(up_proj(), 0)

    for parity in range(2):
        @pl.when((j > 0) & (j < n_hid) & (j % 2 == parity))
        def _(parity=parity):
            us = up_proj()
            o_ref[...] += down_proj(1 - parity)
            conv_gate(us, parity)

    @pl.when(j == n_hid)
    def _():
        gate = mod_ref[row:row + 1, 5 * d:6 * d]
        y = x_ref[...] + gate * (o_ref[...] + down_proj((n_hid - 1) % 2))
        if final_norm:
            y = y * lax.rsqrt(jnp.mean(y * y, axis=-1, keepdims=True) + NORM_EPS) * fg_ref[...]
        o_ref[...] = y


def _conv_ffn(x, norm_g, mods, w_up, ffn_dw, w_down, final_g, *, layer, row, final_norm):
    n, d = x.shape
    hidden = w_down.shape[1]
    tm = _pick_tile(n, 1024)
    th = 512 if hidden % 512 == 0 else 256
    n_tiles, n_hid = n // tm, hidden // th
    hb = tm // HALO
    lyr = lambda i, j: (layer, 0, 0)
    up_tile = lambda j: jnp.minimum(j, n_hid - 1)
    down_tile = lambda j: jnp.maximum(j - 1, 0)
    return pl.pallas_call(
        functools.partial(_ffn_kernel, row=row, d=d, tm=tm, n_tiles=n_tiles, n_hid=n_hid, final_norm=final_norm),
        grid=(n_tiles, n_hid + 1),
        in_specs=[
            pl.BlockSpec((HALO, d), lambda i, j: (jnp.maximum(i * hb - 1, 0), 0)),
            pl.BlockSpec((tm, d), lambda i, j: (i, 0), pipeline_mode=pl.Buffered(1)),
            pl.BlockSpec((HALO, d), lambda i, j: (jnp.minimum((i + 1) * hb, n // HALO - 1), 0)),
            pl.BlockSpec((None, 1, d), lyr),
            pl.BlockSpec((None, SUBLANES_F32, N_MOD * d), lyr),
            pl.BlockSpec((None, d, th), lambda i, j: (layer, 0, up_tile(j))),
            pl.BlockSpec((None, d, th), lambda i, j: (layer, 0, n_hid + up_tile(j))),
            pl.BlockSpec((None, FFN_CONV_KERNEL, th), lambda i, j: (layer, 0, up_tile(j))),
            pl.BlockSpec((None, FFN_CONV_KERNEL, th), lambda i, j: (layer, 0, n_hid + up_tile(j))),
            pl.BlockSpec((None, th, d), lambda i, j: (layer, down_tile(j), 0)),
            pl.BlockSpec((1, d), lambda i, j: (0, 0)),
        ],
        out_specs=pl.BlockSpec((tm, d), lambda i, j: (i, 0)),
        out_shape=jax.ShapeDtypeStruct((n, d), F32),
        scratch_shapes=[pltpu.VMEM((tm + 2 * HALO, d), BF16), pltpu.VMEM((2, tm, th), BF16)],
        compiler_params=_cparams(("parallel", "arbitrary")),
        name="conv_ffn",
    )(x, x, x, norm_g, mods, w_up, w_up, ffn_dw, ffn_dw, w_down, final_g)


def _rope_tables(n_tokens):
    rows = n_tokens // GRID_W
    row = jnp.repeat(jnp.arange(rows), GRID_W).astype(F32)
    col = jnp.tile(jnp.arange(GRID_W), rows).astype(F32)
    n_freq = HEAD_DIM // 4
    inv_freq = ROPE_THETA ** (-jnp.arange(n_freq, dtype=F32) / n_freq)
    ang = jnp.concatenate([row[:, None] * inv_freq, col[:, None] * inv_freq], axis=-1)
    cos, sin = jnp.cos(ang), jnp.sin(ang)
    return jnp.concatenate([cos, cos], axis=-1), jnp.concatenate([-sin, sin], axis=-1)


def kernel(x, c, ctx, c_ctx, w_mod, b_mod, mix_norm, ffn_norm, w_in, q_norm, k_norm, conv_dw, conv_ln_g,
           conv_ln_b, conv_pw, fourier_w, w_out, ffn_up, ffn_dw, ffn_down, final_norm):
    batch, n_lat, d = x.shape
    n_ctx = ctx.shape[1]
    depth = w_mod.shape[0]
    assert batch == 1 and c.shape == (1, d)

    ct = jnp.stack([c[0], c_ctx], axis=1)
    mods = _modulation(ct, w_mod, b_mod)

    w_in_b, w_out_b = w_in.astype(BF16), w_out.astype(BF16)
    ffn_up_b, ffn_down_b = ffn_up.astype(BF16), ffn_down.astype(BF16)
    conv_pw_b, fourier_w_b = conv_pw.astype(BF16), fourier_w.astype(BF16)
    vec = lambda a: a.reshape(depth, 1, a.shape[-1])
    mix_g, ffn_g, qn, kn = vec(mix_norm), vec(ffn_norm), vec(q_norm), vec(k_norm)
    ln_g, ln_b = vec(conv_ln_g), vec(conv_ln_b)
    final_g = final_norm.reshape(1, d)

    rope_l = _rope_tables(n_lat)
    rope_c = (jnp.ones((n_ctx, HEAD_DIM), F32), jnp.zeros((n_ctx, HEAD_DIM), F32))
    tabs_l, tabs_c = _dft_tables(n_lat), _dft_tables(n_ctx)

    xl, xc = x[0], ctx[0]
    for l in range(depth):
        last = l == depth - 1
        qtc, kc, vtc, yc, zrc, zic = _in_projection(xc, mix_g, mods, w_in_b, qn, kn, *rope_c, tabs_c["chan"],
                                                    layer=l, row=1)
        qtl, kl, vtl, yl, zrl, zil = _in_projection(xl, mix_g, mods, w_in_b, qn, kn, *rope_l, tabs_l["chan"],
                                                    layer=l, row=0)
        att = _attention(qtl, kc, vtc, kl, vtl)
        cv = _conv_module(yl, conv_dw, ln_g, ln_b, conv_pw_b, layer=l)
        yf = _position_dft(zrl, zil, tabs_l)
        xl = _out_projection(xl, att, cv, yf, fourier_w_b, w_out_b, mods, layer=l, row=0)
        xl = _conv_ffn(xl, ffn_g, mods, ffn_up_b, ffn_dw, ffn_down_b, final_g, layer=l, row=0, final_norm=last)
        if not last:
            att = _attention(qtc, kc, vtc)
            cv = _conv_module(yc, conv_dw, ln_g, ln_b, conv_pw_b, layer=l)
            yf = _position_dft(zrc, zic, tabs_c)
            xc = _out_projection(xc, att, cv, yf, fourier_w_b, w_out_b, mods, layer=l, row=1)
            xc = _conv_ffn(xc, ffn_g, mods, ffn_up_b, ffn_dw, ffn_down_b, final_g, layer=l, row=1,
                           final_norm=False)
    return xl[None]
```

```python
import functools
import math

import numpy as np
import jax
import jax.numpy as jnp
from jax import lax
from jax.experimental import pallas as pl
from jax.experimental.pallas import tpu as pltpu

F32 = jnp.float32
BF16 = jnp.bfloat16

HEAD_DIM = 128
N_ATTN_HEADS = 8
N_KV_HEADS = 2
GROUP = N_ATTN_HEADS // N_KV_HEADS
ATTN_WIDTH = N_ATTN_HEADS * HEAD_DIM
KV_WIDTH = N_KV_HEADS * HEAD_DIM
CONV_WIDTH = 4 * HEAD_DIM
FOURIER_WIDTH = 4 * HEAD_DIM
N_FOURIER_GROUPS = 4
Q_END = ATTN_WIDTH
K_END = Q_END + KV_WIDTH
V_END = K_END + KV_WIDTH
GLU_MID = V_END + CONV_WIDTH
CONV_END = V_END + 2 * CONV_WIDTH
IN_WIDTH = CONV_END + FOURIER_WIDTH
MIX_WIDTH = ATTN_WIDTH + CONV_WIDTH + FOURIER_WIDTH
CONV_KERNEL = 31
CONV_PAD = (CONV_KERNEL - 1) // 2
FFN_CONV_KERNEL = 3
GRID_W = 64
ROPE_THETA = 10000.0
NORM_EPS = 1e-6
N_MOD = 6
LOG2E = 1.4426950408889634

V7X_VMEM_LIMIT_BYTES = 56 * 1024 * 1024
SUBLANES_F32 = 8
HALO = 16
DFT_N1 = 128
DFT_MIN_N2 = 16
V_ROWS = HEAD_DIM + 16


def _cparams(sem):
    return pltpu.CompilerParams(dimension_semantics=sem, vmem_limit_bytes=V7X_VMEM_LIMIT_BYTES)


def _resident(shape, index_map):
    return pl.BlockSpec(shape, index_map, pipeline_mode=pl.Buffered(1))


def _sigmoid(x):
    return 1.0 / (1.0 + jnp.exp(-x))


def _pick_tile(n, pref):
    t = min(n, pref)
    while n % t:
        t //= 2
    return t


def _mod_kernel(ct_ref, w_ref, b_ref, o_ref):
    ct = ct_ref[...]
    s = ct * _sigmoid(ct)
    w = w_ref[...]
    rows = [jnp.sum(s[:, r:r + 1] * w, axis=0, keepdims=True) for r in range(2)]
    rows.append(jnp.zeros((SUBLANES_F32 - 2, w.shape[1]), F32))
    o_ref[...] = jnp.concatenate(rows, axis=0) + b_ref[...]


def _modulation(ct, w_mod, b_mod):
    depth, d, n = w_mod.shape
    tn = _pick_tile(n, 1024)
    return pl.pallas_call(
        _mod_kernel,
        grid=(depth, n // tn),
        in_specs=[
            pl.BlockSpec((d, 2), lambda l, j: (0, 0)),
            pl.BlockSpec((None, d, tn), lambda l, j: (l, 0, j)),
            pl.BlockSpec((None, 1, tn), lambda l, j: (l, 0, j)),
        ],
        out_specs=pl.BlockSpec((None, SUBLANES_F32, tn), lambda l, j: (l, 0, j)),
        out_shape=jax.ShapeDtypeStruct((depth, SUBLANES_F32, n), F32),
        compiler_params=_cparams(("arbitrary", "arbitrary")),
        name="adaln_modulation",
    )(ct, w_mod, b_mod.reshape(depth, 1, n))


def _head_norm_rope(z, g, cos, sin):
    y = z * lax.rsqrt(jnp.mean(z * z, axis=-1, keepdims=True) + NORM_EPS) * g
    return y * cos + pltpu.roll(y, HEAD_DIM // 2, axis=1) * sin


def _inproj_kernel(x_ref, g_ref, mod_ref, w_ref, qn_ref, kn_ref, cos_ref, sin_ref, dft_ref,
                   qt_ref, k_ref, vt_ref, y_ref, zr_ref, zi_ref, *, row, d, q_scale):
    x = x_ref[...]
    shift = mod_ref[row:row + 1, 0:d]
    scale = mod_ref[row:row + 1, d:2 * d]
    ms = jnp.mean(x * x, axis=-1, keepdims=True)
    h = (x * lax.rsqrt(ms + NORM_EPS) * g_ref[...]) * (1.0 + scale) + shift
    hb = h.astype(BF16)
    cos = cos_ref[...]
    sin = sin_ref[...]

    def proj(lo, hi):
        return jnp.dot(hb, w_ref[:, lo:hi], preferred_element_type=F32)

    for p in range(N_ATTN_HEADS // 2):
        z = proj(2 * p * HEAD_DIM, (2 * p + 2) * HEAD_DIM)
        for j in range(2):
            hd = 2 * p + j
            qh = _head_norm_rope(z[:, j * HEAD_DIM:(j + 1) * HEAD_DIM], qn_ref[...], cos, sin)
            qt_ref[hd * HEAD_DIM:(hd + 1) * HEAD_DIM, :] = (qh * q_scale).T.astype(BF16)
    z = proj(Q_END, K_END)
    for j in range(N_KV_HEADS):
        kh = _head_norm_rope(z[:, j * HEAD_DIM:(j + 1) * HEAD_DIM], kn_ref[...], cos, sin)
        k_ref[:, j * HEAD_DIM:(j + 1) * HEAD_DIM] = kh.astype(BF16)
    vt = proj(K_END, V_END).T
    pad_rows = V_ROWS - HEAD_DIM
    ones_row = (lax.broadcasted_iota(jnp.int32, (pad_rows, vt.shape[1]), 0) == 0).astype(BF16)
    for j in range(N_KV_HEADS):
        vt_ref[j * V_ROWS:j * V_ROWS + HEAD_DIM, :] = vt[j * HEAD_DIM:(j + 1) * HEAD_DIM].astype(BF16)
        vt_ref[j * V_ROWS + HEAD_DIM:(j + 1) * V_ROWS, :] = ones_row
    y_ref[...] = proj(V_END, GLU_MID) * _sigmoid(proj(GLU_MID, CONV_END))
    f = proj(CONV_END, IN_WIDTH)
    for g in range(N_FOURIER_GROUPS):
        lanes = slice(g * HEAD_DIM, (g + 1) * HEAD_DIM)
        r = jnp.dot(f[:, lanes].astype(BF16), dft_ref[...], preferred_element_type=F32)
        zr_ref[:, lanes] = r[:, :HEAD_DIM].astype(BF16)
        zi_ref[:, lanes] = r[:, HEAD_DIM:].astype(BF16)


def _in_projection(x, norm_g, mods, w_in, q_norm, k_norm, cos, sin, dft_c, *, layer, row):
    n, d = x.shape
    tm = _pick_tile(n, 512)
    lyr = lambda i: (layer, 0, 0)
    rows = lambda i: (i, 0)
    kern = functools.partial(_inproj_kernel, row=row, d=d, q_scale=HEAD_DIM ** -0.5 * LOG2E)
    z_spec = pl.BlockSpec((tm, FOURIER_WIDTH), rows)
    z_shape = jax.ShapeDtypeStruct((n, FOURIER_WIDTH), BF16)
    return pl.pallas_call(
        kern,
        grid=(n // tm,),
        in_specs=[
            pl.BlockSpec((tm, d), rows),
            pl.BlockSpec((None, 1, d), lyr),
            pl.BlockSpec((None, SUBLANES_F32, N_MOD * d), lyr),
            _resident((None, d, IN_WIDTH), lyr),
            pl.BlockSpec((None, 1, HEAD_DIM), lyr),
            pl.BlockSpec((None, 1, HEAD_DIM), lyr),
            pl.BlockSpec((tm, HEAD_DIM), rows),
            pl.BlockSpec((tm, HEAD_DIM), rows),
            pl.BlockSpec((HEAD_DIM, 2 * HEAD_DIM), lambda i: (0, 0)),
        ],
        out_specs=[
            pl.BlockSpec((ATTN_WIDTH, tm), lambda i: (0, i)),
            pl.BlockSpec((tm, KV_WIDTH), rows),
            pl.BlockSpec((N_KV_HEADS * V_ROWS, tm), lambda i: (0, i)),
            pl.BlockSpec((tm, CONV_WIDTH), rows),
            z_spec,
            z_spec,
        ],
        out_shape=[
            jax.ShapeDtypeStruct((ATTN_WIDTH, n), BF16),
            jax.ShapeDtypeStruct((n, KV_WIDTH), BF16),
            jax.ShapeDtypeStruct((N_KV_HEADS * V_ROWS, n), BF16),
            jax.ShapeDtypeStruct((n, CONV_WIDTH), F32),
            z_shape,
            z_shape,
        ],
        compiler_params=_cparams(("parallel",)),
        name="in_projection",
    )(x, norm_g, mods, w_in, q_norm, k_norm, cos, sin, dft_c)


def _attn_kernel(*refs, tk, n_lat_chunks, unroll):
    if n_lat_chunks:
        qt_ref, kc_ref, vtc_ref, kl_ref, vtl_ref, o_ref, acc_sc, m_sc, alpha_sc, p_even, p_odd = refs
    else:
        qt_ref, kc_ref, vtc_ref, o_ref = refs
    heads = range(GROUP)

    def q_t(h):
        return qt_ref[h * HEAD_DIM:(h + 1) * HEAD_DIM, :]

    ss = [jnp.dot(kc_ref[...], q_t(h), preferred_element_type=F32) for h in heads]
    ms = [jnp.max(s, axis=0, keepdims=True) for s in ss]
    ps = [jnp.exp2(s - m).astype(BF16) for s, m in zip(ss, ms)]
    accs = [jnp.dot(vtc_ref[...], p, preferred_element_type=F32) for p in ps]

    if n_lat_chunks:
        for h in heads:
            acc_sc[h] = accs[h]
            m_sc[h] = ms[h]
            alpha_sc[h] = jnp.ones_like(ms[h])
        p_bufs = (p_even, p_odd)
        p_odd[...] = jnp.zeros(p_odd.shape, BF16)

        def value_update(c, h, p_src):
            off = pl.multiple_of(c * tk, tk)
            acc_sc[h] = alpha_sc[h] * acc_sc[h] + jnp.dot(vtl_ref[:, pl.ds(off, tk)], p_src[h],
                                                          preferred_element_type=F32)

        def step(c, p_dst, p_src):
            off = pl.multiple_of(c * tk, tk)
            k = kl_ref[pl.ds(off, tk), :]
            ss = [jnp.dot(k, q_t(h), preferred_element_type=F32) for h in heads]
            off_prev = pl.multiple_of(jnp.maximum(c - 1, 0) * tk, tk)
            vt = vtl_ref[:, pl.ds(off_prev, tk)]
            pvs = [jnp.dot(vt, p_src[h], preferred_element_type=F32) for h in heads]
            for h in heads:
                acc_sc[h] = alpha_sc[h] * acc_sc[h] + pvs[h]
                m_old = m_sc[h]
                m_new = jnp.maximum(m_old, jnp.max(ss[h], axis=0, keepdims=True))
                alpha_sc[h] = jnp.exp2(m_old - m_new)
                m_sc[h] = m_new
                p_dst[h] = jnp.exp2(ss[h] - m_new).astype(BF16)

        def body(i, carry):
            for u in range(unroll):
                step(unroll * i + u, p_bufs[u % 2], p_bufs[(u + 1) % 2])
            return carry

        lax.fori_loop(0, n_lat_chunks // unroll, body, 0)
        for h in heads:
            value_update(n_lat_chunks - 1, h, p_odd)
        accs = [acc_sc[h] for h in heads]
    for h in heads:
        out = accs[h][:HEAD_DIM] / accs[h][HEAD_DIM:HEAD_DIM + 1]
        o_ref[:, h * HEAD_DIM:(h + 1) * HEAD_DIM] = out.T.astype(BF16)


def _attention(qt, k_c, vt_c, k_l=None, vt_l=None):
    n = qt.shape[1]
    nc = k_c.shape[0]
    tq = _pick_tile(n, 512)
    qw = GROUP * HEAD_DIM
    in_specs = [
        pl.BlockSpec((qw, tq), lambda k, i: (k, i)),
        pl.BlockSpec((nc, HEAD_DIM), lambda k, i: (0, k)),
        pl.BlockSpec((V_ROWS, nc), lambda k, i: (k, 0)),
    ]
    args = [qt, k_c, vt_c]
    scratch = []
    tk, n_chunks, unroll = 0, 0, 0
    if k_l is not None:
        nl = k_l.shape[0]
        tk = _pick_tile(nl, 512)
        n_chunks = nl // tk
        unroll = 4 if n_chunks % 4 == 0 else 2
        assert n_chunks % unroll == 0, "latent key chunks are processed in pairs"
        in_specs += [
            pl.BlockSpec((nl, HEAD_DIM), lambda k, i: (0, k)),
            pl.BlockSpec((V_ROWS, nl), lambda k, i: (k, 0)),
        ]
        args += [k_l, vt_l]
        scratch = [pltpu.VMEM((GROUP, V_ROWS, tq), F32),
                   pltpu.VMEM((GROUP, 1, tq), F32), pltpu.VMEM((GROUP, 1, tq), F32),
                   pltpu.VMEM((GROUP, tk, tq), BF16), pltpu.VMEM((GROUP, tk, tq), BF16)]
    return pl.pallas_call(
        functools.partial(_attn_kernel, tk=tk, n_lat_chunks=n_chunks, unroll=unroll),
        grid=(N_KV_HEADS, n // tq),
        in_specs=in_specs,
        out_specs=pl.BlockSpec((tq, qw), lambda k, i: (i, k)),
        out_shape=jax.ShapeDtypeStruct((n, ATTN_WIDTH), BF16),
        scratch_shapes=scratch,
        compiler_params=_cparams(("parallel", "parallel")),
        name="attention",
    )(*args)


def _conv_kernel(yp_ref, yc_ref, yn_ref, dw_ref, g_ref, b_ref, pw_ref, o_ref, ext_sc, act_sc,
                 *, tm, n_tiles, chunk):
    i = pl.program_id(0)
    ext_sc[0, 0:HALO, :] = jnp.where(i > 0, yp_ref[...], 0.0)
    ext_sc[0, HALO:HALO + tm, :] = yc_ref[...]
    ext_sc[0, HALO + tm:2 * HALO + tm, :] = jnp.where(i < n_tiles - 1, yn_ref[...], 0.0)
    n_shift_rows = tm + 2 * HALO - SUBLANES_F32
    for s in range(1, SUBLANES_F32):
        ext_sc[s, 0:n_shift_rows, :] = ext_sc[0, s:s + n_shift_rows, :]

    def body(c, carry):
        r0 = pl.multiple_of(c * chunk, chunk)
        acc = jnp.zeros((chunk, CONV_WIDTH), F32)
        for k in range(CONV_KERNEL):
            off = k + HALO - CONV_PAD
            base = pl.multiple_of(r0 + (off // SUBLANES_F32) * SUBLANES_F32, SUBLANES_F32)
            acc = acc + dw_ref[k:k + 1, :] * ext_sc[off % SUBLANES_F32, pl.ds(base, chunk), :]
        mu = jnp.mean(acc, axis=-1, keepdims=True)
        xc = acc - mu
        var = jnp.mean(xc * xc, axis=-1, keepdims=True)
        yn = xc * lax.rsqrt(var + NORM_EPS) * g_ref[...] + b_ref[...]
        act_sc[pl.ds(r0, chunk), :] = (yn * _sigmoid(yn)).astype(BF16)
        return carry

    lax.fori_loop(0, tm // chunk, body, 0)
    o_ref[...] = jnp.dot(act_sc[...], pw_ref[...], preferred_element_type=F32).astype(BF16)


def _conv_module(y, conv_dw, ln_g, ln_b, conv_pw, *, layer):
    n = y.shape[0]
    tm = _pick_tile(n, 256)
    n_tiles = n // tm
    hb = tm // HALO
    lyr = lambda i: (layer, 0, 0)
    chunk = _pick_tile(tm, 64)
    return pl.pallas_call(
        functools.partial(_conv_kernel, tm=tm, n_tiles=n_tiles, chunk=chunk),
        grid=(n_tiles,),
        in_specs=[
            pl.BlockSpec((HALO, CONV_WIDTH), lambda i: (jnp.maximum(i * hb - 1, 0), 0)),
            pl.BlockSpec((tm, CONV_WIDTH), lambda i: (i, 0)),
            pl.BlockSpec((HALO, CONV_WIDTH), lambda i: (jnp.minimum((i + 1) * hb, n // HALO - 1), 0)),
            pl.BlockSpec((None, CONV_KERNEL, CONV_WIDTH), lyr),
            pl.BlockSpec((None, 1, CONV_WIDTH), lyr),
            pl.BlockSpec((None, 1, CONV_WIDTH), lyr),
            pl.BlockSpec((None, CONV_WIDTH, CONV_WIDTH), lyr),
        ],
        out_specs=pl.BlockSpec((tm, CONV_WIDTH), lambda i: (i, 0)),
        out_shape=jax.ShapeDtypeStruct((n, CONV_WIDTH), BF16),
        scratch_shapes=[
            pltpu.VMEM((SUBLANES_F32, tm + 2 * HALO, CONV_WIDTH), F32),
            pltpu.VMEM((tm, CONV_WIDTH), BF16),
        ],
        compiler_params=_cparams(("parallel",)),
        name="conv_module",
    )(y, y, y, conv_dw, ln_g, ln_b, conv_pw)


def _dft_tables(n_pos):
    f32 = lambda a: jnp.asarray(a.astype(np.float32))
    k = np.arange(HEAD_DIM)
    ang = 2.0 * np.pi * np.outer(k, k) / HEAD_DIM
    chan = np.concatenate([np.cos(ang), -np.sin(ang)], axis=1) / math.sqrt(HEAD_DIM)
    tabs = dict(chan=f32(chan).astype(BF16))
    if n_pos < DFT_N1 * DFT_MIN_N2:
        a = 2.0 * np.pi * np.outer(np.arange(n_pos), np.arange(n_pos)) / n_pos
        dense = np.concatenate([np.cos(a), np.sin(a)], axis=1) / math.sqrt(n_pos)
        tabs.update(dense=f32(dense).astype(BF16))
        return tabs
    n1 = DFT_N1
    n2 = n_pos // n1
    a1 = 2.0 * np.pi * np.outer(np.arange(n1), np.arange(n1)) / n1
    c1, s1 = np.cos(a1) / math.sqrt(n1), np.sin(a1) / math.sqrt(n1)
    m1 = np.block([[c1, s1], [-s1, c1]])
    at = 2.0 * np.pi * np.outer(np.arange(n2), np.arange(n1)) / n_pos
    a2 = 2.0 * np.pi * np.outer(np.arange(n2), np.arange(n2)) / n2
    g2 = np.concatenate([np.cos(a2), np.sin(a2)], axis=1) / math.sqrt(n2)
    twc = jnp.broadcast_to(f32(np.cos(at))[:, :, None], (n2, n1, HEAD_DIM))
    tws = jnp.broadcast_to(f32(np.sin(at))[:, :, None], (n2, n1, HEAD_DIM))
    tabs.update(n1=n1, n2=n2, m1=f32(m1).astype(BF16), twc=twc, tws=tws, g2=f32(g2).astype(BF16))
    return tabs


def _dft_stage1_kernel(zr_ref, zi_ref, m1_ref, twc_ref, tws_ref, br_ref, bi_ref, *, n1, nb):
    for b in range(nb):
        cols = slice(b * FOURIER_WIDTH, (b + 1) * FOURIER_WIDTH)
        z = jnp.concatenate([zr_ref[:, cols], zi_ref[:, cols]], axis=0)
        a = jnp.dot(m1_ref[...], z, preferred_element_type=F32)
        tc = twc_ref[b]
        ts = tws_ref[b]
        for g in range(N_FOURIER_GROUPS):
            lanes = slice(g * HEAD_DIM, (g + 1) * HEAD_DIM)
            ar = a[:n1, lanes]
            ai = a[n1:, lanes]
            br_ref[b, :, lanes] = (ar * tc + ai * ts).astype(BF16)
            bi_ref[b, :, lanes] = (ai * tc - ar * ts).astype(BF16)


def _dft_stage2_kernel(br_ref, bi_ref, g2_ref, y_ref):
    b = jnp.concatenate([br_ref[...], bi_ref[...]], axis=0)
    y_ref[...] = jnp.dot(g2_ref[...], b, preferred_element_type=F32).astype(BF16)


def _dft_dense_kernel(zr_ref, zi_ref, g_ref, y_ref):
    z = jnp.concatenate([zr_ref[...], zi_ref[...]], axis=0)
    y_ref[...] = jnp.dot(g_ref[...], z, preferred_element_type=F32).astype(BF16)


def _position_dft(zr, zi, tabs):
    n, w = zr.shape
    if "dense" in tabs:
        return pl.pallas_call(
            _dft_dense_kernel,
            out_shape=jax.ShapeDtypeStruct((n, w), BF16),
            compiler_params=pltpu.CompilerParams(vmem_limit_bytes=V7X_VMEM_LIMIT_BYTES),
            name="position_dft_dense",
        )(zr, zi, tabs["dense"])
    n1, n2 = tabs["n1"], tabs["n2"]
    nb = _pick_tile(n2, 8)
    zr2 = zr.reshape(n1, n2 * w)
    zi2 = zi.reshape(n1, n2 * w)
    br, bi = pl.pallas_call(
        functools.partial(_dft_stage1_kernel, n1=n1, nb=nb),
        grid=(n2 // nb,),
        in_specs=[
            pl.BlockSpec((n1, nb * w), lambda j: (0, j)),
            pl.BlockSpec((n1, nb * w), lambda j: (0, j)),
            pl.BlockSpec((2 * n1, 2 * n1), lambda j: (0, 0)),
            pl.BlockSpec((nb, n1, HEAD_DIM), lambda j: (j, 0, 0)),
            pl.BlockSpec((nb, n1, HEAD_DIM), lambda j: (j, 0, 0)),
        ],
        out_specs=[pl.BlockSpec((nb, n1, w), lambda j: (j, 0, 0))] * 2,
        out_shape=[jax.ShapeDtypeStruct((n2, n1, w), BF16)] * 2,
        compiler_params=_cparams(("parallel",)),
        name="position_dft_stage1",
    )(zr2, zi2, tabs["m1"], tabs["twc"], tabs["tws"])
    tn = _pick_tile(n1 * w, 8192)
    y = pl.pallas_call(
        _dft_stage2_kernel,
        grid=(n1 * w // tn,),
        in_specs=[
            pl.BlockSpec((n2, tn), lambda j: (0, j)),
            pl.BlockSpec((n2, tn), lambda j: (0, j)),
            pl.BlockSpec((n2, 2 * n2), lambda j: (0, 0)),
        ],
        out_specs=pl.BlockSpec((n2, tn), lambda j: (0, j)),
        out_shape=jax.ShapeDtypeStruct((n2, n1 * w), BF16),
        compiler_params=_cparams(("parallel",)),
        name="position_dft_stage2",
    )(br.reshape(n2, n1 * w), bi.reshape(n2, n1 * w), tabs["g2"])
    return y.reshape(n, w)


def _outproj_kernel(x_ref, att_ref, cv_ref, y_ref, fw_ref, wo_ref, mod_ref, o_ref, *, row, d):
    fo = jnp.dot(y_ref[...], fw_ref[...], preferred_element_type=F32).astype(BF16)
    acc = jnp.dot(att_ref[...], wo_ref[0:ATTN_WIDTH, :], preferred_element_type=F32)
    acc = acc + jnp.dot(cv_ref[...], wo_ref[ATTN_WIDTH:ATTN_WIDTH + CONV_WIDTH, :], preferred_element_type=F32)
    acc = acc + jnp.dot(fo, wo_ref[ATTN_WIDTH + CONV_WIDTH:MIX_WIDTH, :], preferred_element_type=F32)
    gate = mod_ref[row:row + 1, 2 * d:3 * d]
    o_ref[...] = x_ref[...] + gate * acc


def _out_projection(x, att, cv, yf, fourier_w, w_out, mods, *, layer, row):
    n, d = x.shape
    tm = _pick_tile(n, 512)
    lyr = lambda i: (layer, 0, 0)
    rows = lambda i: (i, 0)
    return pl.pallas_call(
        functools.partial(_outproj_kernel, row=row, d=d),
        grid=(n // tm,),
        in_specs=[
            pl.BlockSpec((tm, d), rows),
            pl.BlockSpec((tm, ATTN_WIDTH), rows),
            pl.BlockSpec((tm, CONV_WIDTH), rows),
            pl.BlockSpec((tm, FOURIER_WIDTH), rows),
            _resident((None, FOURIER_WIDTH, FOURIER_WIDTH), lyr),
            _resident((None, MIX_WIDTH, d), lyr),
            pl.BlockSpec((None, SUBLANES_F32, N_MOD * d), lyr),
        ],
        out_specs=pl.BlockSpec((tm, d), rows),
        out_shape=jax.ShapeDtypeStruct((n, d), F32),
        compiler_params=_cparams(("parallel",)),
        name="out_projection",
    )(x, att, cv, yf, fourier_w, w_out, mods)


def _ffn_kernel(xp_ref, x_ref, xn_ref, g_ref, mod_ref, wuv_ref, wug_ref, dwv_ref, dwg_ref, wd_ref, fg_ref,
                o_ref, h_sc, silu_sc, uv_sc, *, row, d, tm, n_tiles, n_hid, final_norm):
    i = pl.program_id(0)
    j = pl.program_id(1)

    def conv(u, dw_ref):
        prev = pltpu.roll(u, 1, axis=0)[HALO:HALO + tm]
        nxt = pltpu.roll(u, u.shape[0] - 1, axis=0)[HALO:HALO + tm]
        return dw_ref[0:1, :] * prev + dw_ref[1:2, :] * u[HALO:HALO + tm] + dw_ref[2:3, :] * nxt

    def up_proj():
        h = h_sc[...]
        ug = jnp.dot(h, wug_ref[...], preferred_element_type=F32)
        uv = jnp.dot(h, wuv_ref[...], preferred_element_type=F32)
        return ug, uv

    def stash(ug, uv):
        gt = conv(ug, dwg_ref).astype(BF16)
        silu_sc[...] = gt * _sigmoid(gt)
        uv_sc[...] = uv

    def prev_act():
        return silu_sc[...] * conv(uv_sc[...], dwv_ref).astype(BF16)

    @pl.when(j == 0)
    def _():
        shift = mod_ref[row:row + 1, 3 * d:4 * d]
        scale = mod_ref[row:row + 1, 4 * d:5 * d]

        def norm(x):
            ms = jnp.mean(x * x, axis=-1, keepdims=True)
            return (x * lax.rsqrt(ms + NORM_EPS) * g_ref[...]) * (1.0 + scale) + shift

        h_sc[0:HALO, :] = jnp.where(i > 0, norm(xp_ref[...]), 0.0).astype(BF16)
        h_sc[HALO:HALO + tm, :] = norm(x_ref[...]).astype(BF16)
        h_sc[HALO + tm:2 * HALO + tm, :] = jnp.where(i < n_tiles - 1, norm(xn_ref[...]), 0.0).astype(BF16)
        o_ref[...] = jnp.zeros(o_ref.shape, F32)
        stash(*up_proj())

    @pl.when((j > 0) & (j < n_hid))
    def _():
        act = prev_act()
        ug, uv = up_proj()
        o_ref[...] += jnp.dot(act, wd_ref[...], preferred_element_type=F32)
        stash(ug, uv)

    @pl.when(j == n_hid)
    def _():
        gate = mod_ref[row:row + 1, 5 * d:6 * d]
        y = x_ref[...] + gate * (o_ref[...] + jnp.dot(prev_act(), wd_ref[...], preferred_element_type=F32))
        if final_norm:
            y = y * lax.rsqrt(jnp.mean(y * y, axis=-1, keepdims=True) + NORM_EPS) * fg_ref[...]
        o_ref[...] = y


def _conv_ffn(x, norm_g, mods, w_up, ffn_dw, w_down, final_g, *, layer, row, final_norm):
    n, d = x.shape
    hidden = w_down.shape[1]
    tm = _pick_tile(n, 1024)
    th = 512 if hidden % 512 == 0 else 256
    n_tiles, n_hid = n // tm, hidden // th
    hb = tm // HALO
    lyr = lambda i, j: (layer, 0, 0)
    up_tile = lambda j: jnp.minimum(j, n_hid - 1)
    down_tile = lambda j: jnp.maximum(j - 1, 0)
    return pl.pallas_call(
        functools.partial(_ffn_kernel, row=row, d=d, tm=tm, n_tiles=n_tiles, n_hid=n_hid, final_norm=final_norm),
        grid=(n_tiles, n_hid + 1),
        in_specs=[
            pl.BlockSpec((HALO, d), lambda i, j: (jnp.maximum(i * hb - 1, 0), 0)),
            pl.BlockSpec((tm, d), lambda i, j: (i, 0), pipeline_mode=pl.Buffered(1)),
            pl.BlockSpec((HALO, d), lambda i, j: (jnp.minimum((i + 1) * hb, n // HALO - 1), 0)),
            pl.BlockSpec((None, 1, d), lyr),
            pl.BlockSpec((None, SUBLANES_F32, N_MOD * d), lyr),
            pl.BlockSpec((None, d, th), lambda i, j: (layer, 0, up_tile(j))),
            pl.BlockSpec((None, d, th), lambda i, j: (layer, 0, n_hid + up_tile(j))),
            pl.BlockSpec((None, FFN_CONV_KERNEL, th), lambda i, j: (layer, 0, down_tile(j))),
            pl.BlockSpec((None, FFN_CONV_KERNEL, th), lambda i, j: (layer, 0, n_hid + up_tile(j))),
            pl.BlockSpec((None, th, d), lambda i, j: (layer, down_tile(j), 0)),
            pl.BlockSpec((1, d), lambda i, j: (0, 0)),
        ],
        out_specs=pl.BlockSpec((tm, d), lambda i, j: (i, 0)),
        out_shape=jax.ShapeDtypeStruct((n, d), F32),
        scratch_shapes=[pltpu.VMEM((tm + 2 * HALO, d), BF16), pltpu.VMEM((tm, th), BF16),
                        pltpu.VMEM((tm + 2 * HALO, th), F32)],
        compiler_params=_cparams(("parallel", "arbitrary")),
        name="conv_ffn",
    )(x, x, x, norm_g, mods, w_up, w_up, ffn_dw, ffn_dw, w_down, final_g)


def _rope_tables(n_tokens):
    rows = n_tokens // GRID_W
    row = jnp.repeat(jnp.arange(rows), GRID_W).astype(F32)
    col = jnp.tile(jnp.arange(GRID_W), rows).astype(F32)
    n_freq = HEAD_DIM // 4
    inv_freq = ROPE_THETA ** (-jnp.arange(n_freq, dtype=F32) / n_freq)
    ang = jnp.concatenate([row[:, None] * inv_freq, col[:, None] * inv_freq], axis=-1)
    cos, sin = jnp.cos(ang), jnp.sin(ang)
    return jnp.concatenate([cos, cos], axis=-1), jnp.concatenate([-sin, sin], axis=-1)


def kernel(x, c, ctx, c_ctx, w_mod, b_mod, mix_norm, ffn_norm, w_in, q_norm, k_norm, conv_dw, conv_ln_g,
           conv_ln_b, conv_pw, fourier_w, w_out, ffn_up, ffn_dw, ffn_down, final_norm):
    batch, n_lat, d = x.shape
    n_ctx = ctx.shape[1]
    depth = w_mod.shape[0]
    assert batch == 1 and c.shape == (1, d)

    ct = jnp.stack([c[0], c_ctx], axis=1)
    mods = _modulation(ct, w_mod, b_mod)

    w_in_b, w_out_b = w_in.astype(BF16), w_out.astype(BF16)
    ffn_up_b, ffn_down_b = ffn_up.astype(BF16), ffn_down.astype(BF16)
    conv_pw_b, fourier_w_b = conv_pw.astype(BF16), fourier_w.astype(BF16)
    vec = lambda a: a.reshape(depth, 1, a.shape[-1])
    mix_g, ffn_g, qn, kn = vec(mix_norm), vec(ffn_norm), vec(q_norm), vec(k_norm)
    ln_g, ln_b = vec(conv_ln_g), vec(conv_ln_b)
    final_g = final_norm.reshape(1, d)

    rope_l = _rope_tables(n_lat)
    rope_c = (jnp.ones((n_ctx, HEAD_DIM), F32), jnp.zeros((n_ctx, HEAD_DIM), F32))
    tabs_l, tabs_c = _dft_tables(n_lat), _dft_tables(n_ctx)

    xl, xc = x[0], ctx[0]
    for l in range(depth):
        last = l == depth - 1
        qtc, kc, vtc, yc, zrc, zic = _in_projection(xc, mix_g, mods, w_in_b, qn, kn, *rope_c, tabs_c["chan"],
                                                    layer=l, row=1)
        qtl, kl, vtl, yl, zrl, zil = _in_projection(xl, mix_g, mods, w_in_b, qn, kn, *rope_l, tabs_l["chan"],
                                                    layer=l, row=0)
        att = _attention(qtl, kc, vtc, kl, vtl)
        cv = _conv_module(yl, conv_dw, ln_g, ln_b, conv_pw_b, layer=l)
        yf = _position_dft(zrl, zil, tabs_l)
        xl = _out_projection(xl, att, cv, yf, fourier_w_b, w_out_b, mods, layer=l, row=0)
        xl = _conv_ffn(xl, ffn_g, mods, ffn_up_b, ffn_dw, ffn_down_b, final_g, layer=l, row=0, final_norm=last)
        if not last:
            att = _attention(qtc, kc, vtc)
            cv = _conv_module(yc, conv_dw, ln_g, ln_b, conv_pw_b, layer=l)
            yf = _position_dft(zrc, zic, tabs_c)
            xc = _out_projection(xc, att, cv, yf, fourier_w_b, w_out_b, mods, layer=l, row=1)
            xc = _conv_ffn(xc, ffn_g, mods, ffn_up_b, ffn_dw, ffn_down_b, final_g, layer=l, row=1,
                           final_norm=False)
    return xl[None]
```

```python
import functools
import math

import numpy as np
import jax
import jax.numpy as jnp
from jax import lax
from jax.experimental import pallas as pl
from jax.experimental.pallas import tpu as pltpu

F32 = jnp.float32
BF16 = jnp.bfloat16

HEAD_DIM = 128
N_ATTN_HEADS = 8
N_KV_HEADS = 2
GROUP = N_ATTN_HEADS // N_KV_HEADS
ATTN_WIDTH = N_ATTN_HEADS * HEAD_DIM
KV_WIDTH = N_KV_HEADS * HEAD_DIM
CONV_WIDTH = 4 * HEAD_DIM
FOURIER_WIDTH = 4 * HEAD_DIM
N_FOURIER_GROUPS = 4
Q_END = ATTN_WIDTH
K_END = Q_END + KV_WIDTH
V_END = K_END + KV_WIDTH
GLU_MID = V_END + CONV_WIDTH
CONV_END = V_END + 2 * CONV_WIDTH
IN_WIDTH = CONV_END + FOURIER_WIDTH
MIX_WIDTH = ATTN_WIDTH + CONV_WIDTH + FOURIER_WIDTH
CONV_KERNEL = 31
CONV_PAD = (CONV_KERNEL - 1) // 2
FFN_CONV_KERNEL = 3
GRID_W = 64
ROPE_THETA = 10000.0
NORM_EPS = 1e-6
N_MOD = 6
LOG2E = 1.4426950408889634

V7X_VMEM_LIMIT_BYTES = 56 * 1024 * 1024
SUBLANES_F32 = 8
HALO = 16
DFT_N1 = 128
DFT_MIN_N2 = 16
V_ROWS = HEAD_DIM + 16


def _cparams(sem):
    return pltpu.CompilerParams(dimension_semantics=sem, vmem_limit_bytes=V7X_VMEM_LIMIT_BYTES)


def _resident(shape, index_map):
    return pl.BlockSpec(shape, index_map, pipeline_mode=pl.Buffered(1))


def _sigmoid(x):
    return 1.0 / (1.0 + jnp.exp(-x))


def _pick_tile(n, pref):
    t = min(n, pref)
    while n % t:
        t //= 2
    return t


def _mod_kernel(ct_ref, w_ref, b_ref, o_ref):
    ct = ct_ref[...]
    s = ct * _sigmoid(ct)
    w = w_ref[...]
    rows = [jnp.sum(s[:, r:r + 1] * w, axis=0, keepdims=True) for r in range(2)]
    rows.append(jnp.zeros((SUBLANES_F32 - 2, w.shape[1]), F32))
    o_ref[...] = jnp.concatenate(rows, axis=0) + b_ref[...]


def _modulation(ct, w_mod, b_mod):
    depth, d, n = w_mod.shape
    tn = _pick_tile(n, 1024)
    return pl.pallas_call(
        _mod_kernel,
        grid=(depth, n // tn),
        in_specs=[
            pl.BlockSpec((d, 2), lambda l, j: (0, 0)),
            pl.BlockSpec((None, d, tn), lambda l, j: (l, 0, j)),
            pl.BlockSpec((None, 1, tn), lambda l, j: (l, 0, j)),
        ],
        out_specs=pl.BlockSpec((None, SUBLANES_F32, tn), lambda l, j: (l, 0, j)),
        out_shape=jax.ShapeDtypeStruct((depth, SUBLANES_F32, n), F32),
        compiler_params=_cparams(("arbitrary", "arbitrary")),
        name="adaln_modulation",
    )(ct, w_mod, b_mod.reshape(depth, 1, n))


def _head_norm_rope(z, g, cos, sin):
    y = z * lax.rsqrt(jnp.mean(z * z, axis=-1, keepdims=True) + NORM_EPS) * g
    return y * cos + pltpu.roll(y, HEAD_DIM // 2, axis=1) * sin


def _inproj_kernel(x_ref, g_ref, mod_ref, w_ref, qn_ref, kn_ref, cos_ref, sin_ref, dft_ref,
                   qt_ref, k_ref, vt_ref, y_ref, zr_ref, zi_ref, *, row, d, q_scale):
    x = x_ref[...]
    shift = mod_ref[row:row + 1, 0:d]
    scale = mod_ref[row:row + 1, d:2 * d]
    ms = jnp.mean(x * x, axis=-1, keepdims=True)
    h = (x * lax.rsqrt(ms + NORM_EPS) * g_ref[...]) * (1.0 + scale) + shift
    hb = h.astype(BF16)
    cos = cos_ref[...]
    sin = sin_ref[...]

    def proj(lo, hi):
        return jnp.dot(hb, w_ref[:, lo:hi], preferred_element_type=F32)

    for p in range(N_ATTN_HEADS // 2):
        z = proj(2 * p * HEAD_DIM, (2 * p + 2) * HEAD_DIM)
        for j in range(2):
            hd = 2 * p + j
            qh = _head_norm_rope(z[:, j * HEAD_DIM:(j + 1) * HEAD_DIM], qn_ref[...], cos, sin)
            qt_ref[hd * HEAD_DIM:(hd + 1) * HEAD_DIM, :] = (qh * q_scale).T.astype(BF16)
    z = proj(Q_END, K_END)
    for j in range(N_KV_HEADS):
        kh = _head_norm_rope(z[:, j * HEAD_DIM:(j + 1) * HEAD_DIM], kn_ref[...], cos, sin)
        k_ref[:, j * HEAD_DIM:(j + 1) * HEAD_DIM] = kh.astype(BF16)
    vt = proj(K_END, V_END).T
    pad_rows = V_ROWS - HEAD_DIM
    ones_row = (lax.broadcasted_iota(jnp.int32, (pad_rows, vt.shape[1]), 0) == 0).astype(BF16)
    for j in range(N_KV_HEADS):
        vt_ref[j * V_ROWS:j * V_ROWS + HEAD_DIM, :] = vt[j * HEAD_DIM:(j + 1) * HEAD_DIM].astype(BF16)
        vt_ref[j * V_ROWS + HEAD_DIM:(j + 1) * V_ROWS, :] = ones_row
    y_ref[...] = proj(V_END, GLU_MID) * _sigmoid(proj(GLU_MID, CONV_END))
    f = proj(CONV_END, IN_WIDTH)
    for g in range(N_FOURIER_GROUPS):
        lanes = slice(g * HEAD_DIM, (g + 1) * HEAD_DIM)
        r = jnp.dot(f[:, lanes].astype(BF16), dft_ref[...], preferred_element_type=F32)
        zr_ref[:, lanes] = r[:, :HEAD_DIM].astype(BF16)
        zi_ref[:, lanes] = r[:, HEAD_DIM:].astype(BF16)


def _in_projection(x, norm_g, mods, w_in, q_norm, k_norm, cos, sin, dft_c, *, layer, row):
    n, d = x.shape
    tm = _pick_tile(n, 512)
    lyr = lambda i: (layer, 0, 0)
    rows = lambda i: (i, 0)
    kern = functools.partial(_inproj_kernel, row=row, d=d, q_scale=HEAD_DIM ** -0.5 * LOG2E)
    z_spec = pl.BlockSpec((tm, FOURIER_WIDTH), rows)
    z_shape = jax.ShapeDtypeStruct((n, FOURIER_WIDTH), BF16)
    return pl.pallas_call(
        kern,
        grid=(n // tm,),
        in_specs=[
            pl.BlockSpec((tm, d), rows),
            pl.BlockSpec((None, 1, d), lyr),
            pl.BlockSpec((None, SUBLANES_F32, N_MOD * d), lyr),
            _resident((None, d, IN_WIDTH), lyr),
            pl.BlockSpec((None, 1, HEAD_DIM), lyr),
            pl.BlockSpec((None, 1, HEAD_DIM), lyr),
            pl.BlockSpec((tm, HEAD_DIM), rows),
            pl.BlockSpec((tm, HEAD_DIM), rows),
            pl.BlockSpec((HEAD_DIM, 2 * HEAD_DIM), lambda i: (0, 0)),
        ],
        out_specs=[
            pl.BlockSpec((ATTN_WIDTH, tm), lambda i: (0, i)),
            pl.BlockSpec((tm, KV_WIDTH), rows),
            pl.BlockSpec((N_KV_HEADS * V_ROWS, tm), lambda i: (0, i)),
            pl.BlockSpec((tm, CONV_WIDTH), rows),
            z_spec,
            z_spec,
        ],
        out_shape=[
            jax.ShapeDtypeStruct((ATTN_WIDTH, n), BF16),
            jax.ShapeDtypeStruct((n, KV_WIDTH), BF16),
            jax.ShapeDtypeStruct((N_KV_HEADS * V_ROWS, n), BF16),
            jax.ShapeDtypeStruct((n, CONV_WIDTH), F32),
            z_shape,
            z_shape,
        ],
        compiler_params=_cparams(("parallel",)),
        name="in_projection",
    )(x, norm_g, mods, w_in, q_norm, k_norm, cos, sin, dft_c)


def _attn_kernel(*refs, tk, n_lat_chunks, unroll):
    if n_lat_chunks:
        qt_ref, kc_ref, vtc_ref, kl_ref, vtl_ref, o_ref, acc_sc, m_sc, alpha_sc, p_even, p_odd = refs
    else:
        qt_ref, kc_ref, vtc_ref, o_ref = refs
    heads = range(GROUP)

    def q_t(h):
        return qt_ref[h * HEAD_DIM:(h + 1) * HEAD_DIM, :]

    ss = [jnp.dot(kc_ref[...], q_t(h), preferred_element_type=F32) for h in heads]
    ms = [jnp.max(s, axis=0, keepdims=True) for s in ss]
    ps = [jnp.exp2(s - m).astype(BF16) for s, m in zip(ss, ms)]
    accs = [jnp.dot(vtc_ref[...], p, preferred_element_type=F32) for p in ps]

    if n_lat_chunks:
        for h in heads:
            acc_sc[h] = accs[h]
            m_sc[h] = ms[h]
            alpha_sc[h] = jnp.ones_like(ms[h])
        p_bufs = (p_even, p_odd)
        p_odd[...] = jnp.zeros(p_odd.shape, BF16)

        def value_update(c, h, p_src):
            off = pl.multiple_of(c * tk, tk)
            acc_sc[h] = alpha_sc[h] * acc_sc[h] + jnp.dot(vtl_ref[:, pl.ds(off, tk)], p_src[h],
                                                          preferred_element_type=F32)

        def step(c, p_dst, p_src):
            off = pl.multiple_of(c * tk, tk)
            k = kl_ref[pl.ds(off, tk), :]
            ss = [jnp.dot(k, q_t(h), preferred_element_type=F32) for h in heads]
            off_prev = pl.multiple_of(jnp.maximum(c - 1, 0) * tk, tk)
            vt = vtl_ref[:, pl.ds(off_prev, tk)]
            pvs = [jnp.dot(vt, p_src[h], preferred_element_type=F32) for h in heads]
            for h in heads:
                acc_sc[h] = alpha_sc[h] * acc_sc[h] + pvs[h]
                m_old = m_sc[h]
                m_new = jnp.maximum(m_old, jnp.max(ss[h], axis=0, keepdims=True))
                alpha_sc[h] = jnp.exp2(m_old - m_new)
                m_sc[h] = m_new
                p_dst[h] = jnp.exp2(ss[h] - m_new).astype(BF16)

        def body(i, carry):
            for u in range(unroll):
                step(unroll * i + u, p_bufs[u % 2], p_bufs[(u + 1) % 2])
            return carry

        lax.fori_loop(0, n_lat_chunks // unroll, body, 0)
        for h in heads:
            value_update(n_lat_chunks - 1, h, p_odd)
        accs = [acc_sc[h] for h in heads]
    for h in heads:
        out = accs[h][:HEAD_DIM] / accs[h][HEAD_DIM:HEAD_DIM + 1]
        o_ref[:, h * HEAD_DIM:(h + 1) * HEAD_DIM] = out.T.astype(BF16)


def _attention(qt, k_c, vt_c, k_l=None, vt_l=None):
    n = qt.shape[1]
    nc = k_c.shape[0]
    tq = _pick_tile(n, 512)
    qw = GROUP * HEAD_DIM
    in_specs = [
        pl.BlockSpec((qw, tq), lambda k, i: (k, i)),
        pl.BlockSpec((nc, HEAD_DIM), lambda k, i: (0, k)),
        pl.BlockSpec((V_ROWS, nc), lambda k, i: (k, 0)),
    ]
    args = [qt, k_c, vt_c]
    scratch = []
    tk, n_chunks, unroll = 0, 0, 0
    if k_l is not None:
        nl = k_l.shape[0]
        tk = _pick_tile(nl, 512)
        n_chunks = nl // tk
        unroll = 4 if n_chunks % 4 == 0 else 2
        assert n_chunks % unroll == 0, "latent key chunks are processed in pairs"
        in_specs += [
            pl.BlockSpec((nl, HEAD_DIM), lambda k, i: (0, k)),
            pl.BlockSpec((V_ROWS, nl), lambda k, i: (k, 0)),
        ]
        args += [k_l, vt_l]
        scratch = [pltpu.VMEM((GROUP, V_ROWS, tq), F32),
                   pltpu.VMEM((GROUP, 1, tq), F32), pltpu.VMEM((GROUP, 1, tq), F32),
                   pltpu.VMEM((GROUP, tk, tq), BF16), pltpu.VMEM((GROUP, tk, tq), BF16)]
    return pl.pallas_call(
        functools.partial(_attn_kernel, tk=tk, n_lat_chunks=n_chunks, unroll=unroll),
        grid=(N_KV_HEADS, n // tq),
        in_specs=in_specs,
        out_specs=pl.BlockSpec((tq, qw), lambda k, i: (i, k)),
        out_shape=jax.ShapeDtypeStruct((n, ATTN_WIDTH), BF16),
        scratch_shapes=scratch,
        compiler_params=_cparams(("parallel", "parallel")),
        name="attention",
    )(*args)


def _conv_kernel(yp_ref, yc_ref, yn_ref, dw_ref, g_ref, b_ref, pw_ref, o_ref, ext_sc, act_sc,
                 *, tm, n_tiles, chunk):
    i = pl.program_id(0)
    ext_sc[0, 0:HALO, :] = jnp.where(i > 0, yp_ref[...], 0.0)
    ext_sc[0, HALO:HALO + tm, :] = yc_ref[...]
    ext_sc[0, HALO + tm:2 * HALO + tm, :] = jnp.where(i < n_tiles - 1, yn_ref[...], 0.0)
    n_shift_rows = tm + 2 * HALO - SUBLANES_F32
    for s in range(1, SUBLANES_F32):
        ext_sc[s, 0:n_shift_rows, :] = ext_sc[0, s:s + n_shift_rows, :]

    def body(c, carry):
        r0 = pl.multiple_of(c * chunk, chunk)
        acc = jnp.zeros((chunk, CONV_WIDTH), F32)
        for k in range(CONV_KERNEL):
            off = k + HALO - CONV_PAD
            base = pl.multiple_of(r0 + (off // SUBLANES_F32) * SUBLANES_F32, SUBLANES_F32)
            acc = acc + dw_ref[k:k + 1, :] * ext_sc[off % SUBLANES_F32, pl.ds(base, chunk), :]
        mu = jnp.mean(acc, axis=-1, keepdims=True)
        xc = acc - mu
        var = jnp.mean(xc * xc, axis=-1, keepdims=True)
        yn = xc * lax.rsqrt(var + NORM_EPS) * g_ref[...] + b_ref[...]
        act_sc[pl.ds(r0, chunk), :] = (yn * _sigmoid(yn)).astype(BF16)
        return carry

    lax.fori_loop(0, tm // chunk, body, 0)
    o_ref[...] = jnp.dot(act_sc[...], pw_ref[...], preferred_element_type=F32).astype(BF16)


def _conv_module(y, conv_dw, ln_g, ln_b, conv_pw, *, layer):
    n = y.shape[0]
    tm = _pick_tile(n, 256)
    n_tiles = n // tm
    hb = tm // HALO
    lyr = lambda i: (layer, 0, 0)
    chunk = _pick_tile(tm, 256)
    return pl.pallas_call(
        functools.partial(_conv_kernel, tm=tm, n_tiles=n_tiles, chunk=chunk),
        grid=(n_tiles,),
        in_specs=[
            pl.BlockSpec((HALO, CONV_WIDTH), lambda i: (jnp.maximum(i * hb - 1, 0), 0)),
            pl.BlockSpec((tm, CONV_WIDTH), lambda i: (i, 0)),
            pl.BlockSpec((HALO, CONV_WIDTH), lambda i: (jnp.minimum((i + 1) * hb, n // HALO - 1), 0)),
            pl.BlockSpec((None, CONV_KERNEL, CONV_WIDTH), lyr),
            pl.BlockSpec((None, 1, CONV_WIDTH), lyr),
            pl.BlockSpec((None, 1, CONV_WIDTH), lyr),
            pl.BlockSpec((None, CONV_WIDTH, CONV_WIDTH), lyr),
        ],
        out_specs=pl.BlockSpec((tm, CONV_WIDTH), lambda i: (i, 0)),
        out_shape=jax.ShapeDtypeStruct((n, CONV_WIDTH), BF16),
        scratch_shapes=[
            pltpu.VMEM((SUBLANES_F32, tm + 2 * HALO, CONV_WIDTH), F32),
            pltpu.VMEM((tm, CONV_WIDTH), BF16),
        ],
        compiler_params=_cparams(("parallel",)),
        name="conv_module",
    )(y, y, y, conv_dw, ln_g, ln_b, conv_pw)


def _dft_tables(n_pos):
    f32 = lambda a: jnp.asarray(a.astype(np.float32))
    k = np.arange(HEAD_DIM)
    ang = 2.0 * np.pi * np.outer(k, k) / HEAD_DIM
    chan = np.concatenate([np.cos(ang), -np.sin(ang)], axis=1) / math.sqrt(HEAD_DIM)
    tabs = dict(chan=f32(chan).astype(BF16))
    if n_pos < DFT_N1 * DFT_MIN_N2:
        a = 2.0 * np.pi * np.outer(np.arange(n_pos), np.arange(n_pos)) / n_pos
        dense = np.concatenate([np.cos(a), np.sin(a)], axis=1) / math.sqrt(n_pos)
        tabs.update(dense=f32(dense).astype(BF16))
        return tabs
    n1 = DFT_N1
    n2 = n_pos // n1
    a1 = 2.0 * np.pi * np.outer(np.arange(n1), np.arange(n1)) / n1
    c1, s1 = np.cos(a1) / math.sqrt(n1), np.sin(a1) / math.sqrt(n1)
    m1 = np.block([[c1, s1], [-s1, c1]])
    at = 2.0 * np.pi * np.outer(np.arange(n2), np.arange(n1)) / n_pos
    a2 = 2.0 * np.pi * np.outer(np.arange(n2), np.arange(n2)) / n2
    g2 = np.concatenate([np.cos(a2), np.sin(a2)], axis=1) / math.sqrt(n2)
    twc = jnp.broadcast_to(f32(np.cos(at))[:, :, None], (n2, n1, HEAD_DIM))
    tws = jnp.broadcast_to(f32(np.sin(at))[:, :, None], (n2, n1, HEAD_DIM))
    tabs.update(n1=n1, n2=n2, m1=f32(m1).astype(BF16), twc=twc, tws=tws, g2=f32(g2).astype(BF16))
    return tabs


def _dft_stage1_kernel(zr_ref, zi_ref, m1_ref, twc_ref, tws_ref, br_ref, bi_ref, *, n1, nb):
    for b in range(nb):
        cols = slice(b * FOURIER_WIDTH, (b + 1) * FOURIER_WIDTH)
        z = jnp.concatenate([zr_ref[:, cols], zi_ref[:, cols]], axis=0)
        a = jnp.dot(m1_ref[...], z, preferred_element_type=F32)
        tc = twc_ref[b]
        ts = tws_ref[b]
        for g in range(N_FOURIER_GROUPS):
            lanes = slice(g * HEAD_DIM, (g + 1) * HEAD_DIM)
            ar = a[:n1, lanes]
            ai = a[n1:, lanes]
            br_ref[b, :, lanes] = (ar * tc + ai * ts).astype(BF16)
            bi_ref[b, :, lanes] = (ai * tc - ar * ts).astype(BF16)


def _dft_stage2_kernel(br_ref, bi_ref, g2_ref, y_ref):
    b = jnp.concatenate([br_ref[...], bi_ref[...]], axis=0)
    y_ref[...] = jnp.dot(g2_ref[...], b, preferred_element_type=F32).astype(BF16)


def _dft_dense_kernel(zr_ref, zi_ref, g_ref, y_ref):
    z = jnp.concatenate([zr_ref[...], zi_ref[...]], axis=0)
    y_ref[...] = jnp.dot(g_ref[...], z, preferred_element_type=F32).astype(BF16)


def _position_dft(zr, zi, tabs):
    n, w = zr.shape
    if "dense" in tabs:
        return pl.pallas_call(
            _dft_dense_kernel,
            out_shape=jax.ShapeDtypeStruct((n, w), BF16),
            compiler_params=pltpu.CompilerParams(vmem_limit_bytes=V7X_VMEM_LIMIT_BYTES),
            name="position_dft_dense",
        )(zr, zi, tabs["dense"])
    n1, n2 = tabs["n1"], tabs["n2"]
    nb = _pick_tile(n2, 8)
    zr2 = zr.reshape(n1, n2 * w)
    zi2 = zi.reshape(n1, n2 * w)
    br, bi = pl.pallas_call(
        functools.partial(_dft_stage1_kernel, n1=n1, nb=nb),
        grid=(n2 // nb,),
        in_specs=[
            pl.BlockSpec((n1, nb * w), lambda j: (0, j)),
            pl.BlockSpec((n1, nb * w), lambda j: (0, j)),
            pl.BlockSpec((2 * n1, 2 * n1), lambda j: (0, 0)),
            pl.BlockSpec((nb, n1, HEAD_DIM), lambda j: (j, 0, 0)),
            pl.BlockSpec((nb, n1, HEAD_DIM), lambda j: (j, 0, 0)),
        ],
        out_specs=[pl.BlockSpec((nb, n1, w), lambda j: (j, 0, 0))] * 2,
        out_shape=[jax.ShapeDtypeStruct((n2, n1, w), BF16)] * 2,
        compiler_params=_cparams(("parallel",)),
        name="position_dft_stage1",
    )(zr2, zi2, tabs["m1"], tabs["twc"], tabs["tws"])
    tn = _pick_tile(n1 * w, 8192)
    y = pl.pallas_call(
        _dft_stage2_kernel,
        grid=(n1 * w // tn,),
        in_specs=[
            pl.BlockSpec((n2, tn), lambda j: (0, j)),
            pl.BlockSpec((n2, tn), lambda j: (0, j)),
            pl.BlockSpec((n2, 2 * n2), lambda j: (0, 0)),
        ],
        out_specs=pl.BlockSpec((n2, tn), lambda j: (0, j)),
        out_shape=jax.ShapeDtypeStruct((n2, n1 * w), BF16),
        compiler_params=_cparams(("parallel",)),
        name="position_dft_stage2",
    )(br.reshape(n2, n1 * w), bi.reshape(n2, n1 * w), tabs["g2"])
    return y.reshape(n, w)


def _outproj_kernel(x_ref, att_ref, cv_ref, y_ref, fw_ref, wo_ref, mod_ref, o_ref, *, row, d):
    fo = jnp.dot(y_ref[...], fw_ref[...], preferred_element_type=F32).astype(BF16)
    acc = jnp.dot(att_ref[...], wo_ref[0:ATTN_WIDTH, :], preferred_element_type=F32)
    acc = acc + jnp.dot(cv_ref[...], wo_ref[ATTN_WIDTH:ATTN_WIDTH + CONV_WIDTH, :], preferred_element_type=F32)
    acc = acc + jnp.dot(fo, wo_ref[ATTN_WIDTH + CONV_WIDTH:MIX_WIDTH, :], preferred_element_type=F32)
    gate = mod_ref[row:row + 1, 2 * d:3 * d]
    o_ref[...] = x_ref[...] + gate * acc


def _out_projection(x, att, cv, yf, fourier_w, w_out, mods, *, layer, row):
    n, d = x.shape
    tm = _pick_tile(n, 512)
    lyr = lambda i: (layer, 0, 0)
    rows = lambda i: (i, 0)
    return pl.pallas_call(
        functools.partial(_outproj_kernel, row=row, d=d),
        grid=(n // tm,),
        in_specs=[
            pl.BlockSpec((tm, d), rows),
            pl.BlockSpec((tm, ATTN_WIDTH), rows),
            pl.BlockSpec((tm, CONV_WIDTH), rows),
            pl.BlockSpec((tm, FOURIER_WIDTH), rows),
            _resident((None, FOURIER_WIDTH, FOURIER_WIDTH), lyr),
            _resident((None, MIX_WIDTH, d), lyr),
            pl.BlockSpec((None, SUBLANES_F32, N_MOD * d), lyr),
        ],
        out_specs=pl.BlockSpec((tm, d), rows),
        out_shape=jax.ShapeDtypeStruct((n, d), F32),
        compiler_params=_cparams(("parallel",)),
        name="out_projection",
    )(x, att, cv, yf, fourier_w, w_out, mods)


def _ffn_kernel(xp_ref, x_ref, xn_ref, g_ref, mod_ref, wuv_ref, wug_ref, dwv_ref, dwg_ref, wd_ref, fg_ref,
                o_ref, h_sc, silu_sc, uv_sc, *, row, d, tm, n_tiles, n_hid, final_norm):
    i = pl.program_id(0)
    j = pl.program_id(1)

    def conv(u, dw_ref):
        prev = pltpu.roll(u, 1, axis=0)[HALO:HALO + tm]
        nxt = pltpu.roll(u, u.shape[0] - 1, axis=0)[HALO:HALO + tm]
        return dw_ref[0:1, :] * prev + dw_ref[1:2, :] * u[HALO:HALO + tm] + dw_ref[2:3, :] * nxt

    def up_proj():
        h = h_sc[...]
        ug = jnp.dot(h, wug_ref[...], preferred_element_type=F32)
        uv = jnp.dot(h, wuv_ref[...], preferred_element_type=F32)
        return ug, uv

    def stash(ug, uv):
        gt = conv(ug, dwg_ref).astype(BF16)
        silu_sc[...] = gt * _sigmoid(gt)
        uv_sc[...] = uv

    def prev_act():
        return silu_sc[...] * conv(uv_sc[...], dwv_ref).astype(BF16)

    @pl.when(j == 0)
    def _():
        shift = mod_ref[row:row + 1, 3 * d:4 * d]
        scale = mod_ref[row:row + 1, 4 * d:5 * d]

        def norm(x):
            ms = jnp.mean(x * x, axis=-1, keepdims=True)
            return (x * lax.rsqrt(ms + NORM_EPS) * g_ref[...]) * (1.0 + scale) + shift

        h_sc[0:HALO, :] = jnp.where(i > 0, norm(xp_ref[...]), 0.0).astype(BF16)
        h_sc[HALO:HALO + tm, :] = norm(x_ref[...]).astype(BF16)
        h_sc[HALO + tm:2 * HALO + tm, :] = jnp.where(i < n_tiles - 1, norm(xn_ref[...]), 0.0).astype(BF16)
        o_ref[...] = jnp.zeros(o_ref.shape, F32)
        stash(*up_proj())

    @pl.when((j > 0) & (j < n_hid))
    def _():
        act = prev_act()
        ug, uv = up_proj()
        o_ref[...] += jnp.dot(act, wd_ref[...], preferred_element_type=F32)
        stash(ug, uv)

    @pl.when(j == n_hid)
    def _():
        gate = mod_ref[row:row + 1, 5 * d:6 * d]
        y = x_ref[...] + gate * (o_ref[...] + jnp.dot(prev_act(), wd_ref[...], preferred_element_type=F32))
        if final_norm:
            y = y * lax.rsqrt(jnp.mean(y * y, axis=-1, keepdims=True) + NORM_EPS) * fg_ref[...]
        o_ref[...] = y


def _conv_ffn(x, norm_g, mods, w_up, ffn_dw, w_down, final_g, *, layer, row, final_norm):
    n, d = x.shape
    hidden = w_down.shape[1]
    tm = _pick_tile(n, 1024)
    th = 512 if hidden % 512 == 0 else 256
    n_tiles, n_hid = n // tm, hidden // th
    hb = tm // HALO
    lyr = lambda i, j: (layer, 0, 0)
    up_tile = lambda j: jnp.minimum(j, n_hid - 1)
    down_tile = lambda j: jnp.maximum(j - 1, 0)
    return pl.pallas_call(
        functools.partial(_ffn_kernel, row=row, d=d, tm=tm, n_tiles=n_tiles, n_hid=n_hid, final_norm=final_norm),
        grid=(n_tiles, n_hid + 1),
        in_specs=[
            pl.BlockSpec((HALO, d), lambda i, j: (jnp.maximum(i * hb - 1, 0), 0)),
            pl.BlockSpec((tm, d), lambda i, j: (i, 0), pipeline_mode=pl.Buffered(1)),
            pl.BlockSpec((HALO, d), lambda i, j: (jnp.minimum((i + 1) * hb, n // HALO - 1), 0)),
            pl.BlockSpec((None, 1, d), lyr),
            pl.BlockSpec((None, SUBLANES_F32, N_MOD * d), lyr),
            pl.BlockSpec((None, d, th), lambda i, j: (layer, 0, up_tile(j))),
            pl.BlockSpec((None, d, th), lambda i, j: (layer, 0, n_hid + up_tile(j))),
            pl.BlockSpec((None, FFN_CONV_KERNEL, th), lambda i, j: (layer, 0, down_tile(j))),
            pl.BlockSpec((None, FFN_CONV_KERNEL, th), lambda i, j: (layer, 0, n_hid + up_tile(j))),
            pl.BlockSpec((None, th, d), lambda i, j: (layer, down_tile(j), 0)),
            pl.BlockSpec((1, d), lambda i, j: (0, 0)),
        ],
        out_specs=pl.BlockSpec((tm, d), lambda i, j: (i, 0)),
        out_shape=jax.ShapeDtypeStruct((n, d), F32),
        scratch_shapes=[pltpu.VMEM((tm + 2 * HALO, d), BF16), pltpu.VMEM((tm, th), BF16),
                        pltpu.VMEM((tm + 2 * HALO, th), F32)],
        compiler_params=_cparams(("parallel", "arbitrary")),
        name="conv_ffn",
    )(x, x, x, norm_g, mods, w_up, w_up, ffn_dw, ffn_dw, w_down, final_g)


def _rope_tables(n_tokens):
    rows = n_tokens // GRID_W
    row = jnp.repeat(jnp.arange(rows), GRID_W).astype(F32)
    col = jnp.tile(jnp.arange(GRID_W), rows).astype(F32)
    n_freq = HEAD_DIM // 4
    inv_freq = ROPE_THETA ** (-jnp.arange(n_freq, dtype=F32) / n_freq)
    ang = jnp.concatenate([row[:, None] * inv_freq, col[:, None] * inv_freq], axis=-1)
    cos, sin = jnp.cos(ang), jnp.sin(ang)
    return jnp.concatenate([cos, cos], axis=-1), jnp.concatenate([-sin, sin], axis=-1)


def kernel(x, c, ctx, c_ctx, w_mod, b_mod, mix_norm, ffn_norm, w_in, q_norm, k_norm, conv_dw, conv_ln_g,
           conv_ln_b, conv_pw, fourier_w, w_out, ffn_up, ffn_dw, ffn_down, final_norm):
    batch, n_lat, d = x.shape
    n_ctx = ctx.shape[1]
    depth = w_mod.shape[0]
    assert batch == 1 and c.shape == (1, d)

    ct = jnp.stack([c[0], c_ctx], axis=1)
    mods = _modulation(ct, w_mod, b_mod)

    w_in_b, w_out_b = w_in.astype(BF16), w_out.astype(BF16)
    ffn_up_b, ffn_down_b = ffn_up.astype(BF16), ffn_down.astype(BF16)
    conv_pw_b, fourier_w_b = conv_pw.astype(BF16), fourier_w.astype(BF16)
    vec = lambda a: a.reshape(depth, 1, a.shape[-1])
    mix_g, ffn_g, qn, kn = vec(mix_norm), vec(ffn_norm), vec(q_norm), vec(k_norm)
    ln_g, ln_b = vec(conv_ln_g), vec(conv_ln_b)
    final_g = final_norm.reshape(1, d)

    rope_l = _rope_tables(n_lat)
    rope_c = (jnp.ones((n_ctx, HEAD_DIM), F32), jnp.zeros((n_ctx, HEAD_DIM), F32))
    tabs_l, tabs_c = _dft_tables(n_lat), _dft_tables(n_ctx)

    xl, xc = x[0], ctx[0]
    for l in range(depth):
        last = l == depth - 1
        qtc, kc, vtc, yc, zrc, zic = _in_projection(xc, mix_g, mods, w_in_b, qn, kn, *rope_c, tabs_c["chan"],
                                                    layer=l, row=1)
        qtl, kl, vtl, yl, zrl, zil = _in_projection(xl, mix_g, mods, w_in_b, qn, kn, *rope_l, tabs_l["chan"],
                                                    layer=l, row=0)
        att = _attention(qtl, kc, vtc, kl, vtl)
        cv = _conv_module(yl, conv_dw, ln_g, ln_b, conv_pw_b, layer=l)
        yf = _position_dft(zrl, zil, tabs_l)
        xl = _out_projection(xl, att, cv, yf, fourier_w_b, w_out_b, mods, layer=l, row=0)
        xl = _conv_ffn(xl, ffn_g, mods, ffn_up_b, ffn_dw, ffn_down_b, final_g, layer=l, row=0, final_norm=last)
        if not last:
            att = _attention(qtc, kc, vtc)
            cv = _conv_module(yc, conv_dw, ln_g, ln_b, conv_pw_b, layer=l)
            yf = _position_dft(zrc, zic, tabs_c)
            xc = _out_projection(xc, att, cv, yf, fourier_w_b, w_out_b, mods, layer=l, row=1)
            xc = _conv_ffn(xc, ffn_g, mods, ffn_up_b, ffn_dw, ffn_down_b, final_g, layer=l, row=1,
                           final_norm=False)
    return xl[None]
```
